```python
import math
import jax
import jax.numpy as jnp
from jax import lax
import numpy as np

D_MODEL = 1024
BATCH = 4
SEQ = 4096
DEPTH = 2
DEC_BATCH = 32
DEC_SEQ = 64
PAST_LEN = 4096

CHUNK = 64
SB_HEAD_DIM = 64
SB_HEADS = D_MODEL // 128
SB_WIDTH = SB_HEADS * SB_HEAD_DIM
SB_QBLOCK = 128
LRU_WIDTH = D_MODEL // 2
LRU_BLOCKS = 8
LRU_BLOCK_DIM = LRU_WIDTH // LRU_BLOCKS
LRU_C = 8.0
CONV_W = 4
SSM_INNER = D_MODEL // 2
SSM_HEAD_DIM = 64
SSM_HEADS = SSM_INNER // SSM_HEAD_DIM
SSM_GROUPS = 2
SSM_HPG = SSM_HEADS // SSM_GROUPS
SSM_STATE = 128
SSM_CONV_DIM = SSM_INNER + 2 * SSM_GROUPS * SSM_STATE
MIX_WIDTH = SB_WIDTH + LRU_WIDTH + SSM_INNER
N_BRANCH = 3
IN_SPLITS = (SB_WIDTH, 2 * SB_WIDTH, 3 * SB_WIDTH,
             3 * SB_WIDTH + LRU_WIDTH, 3 * SB_WIDTH + 2 * LRU_WIDTH,
             3 * SB_WIDTH + 2 * LRU_WIDTH + SSM_INNER,
             3 * SB_WIDTH + 2 * LRU_WIDTH + SSM_INNER + SSM_CONV_DIM)
IN_PROJ_WIDTH = IN_SPLITS[-1] + SSM_HEADS
N_EXPERT_GROUPS = 4
EXPERTS_PER_GROUP = 8
N_EXPERTS = N_EXPERT_GROUPS * EXPERTS_PER_GROUP
TOP_K = 2
D_EXPERT = D_MODEL // 2
MOE_BLOCK = 128
EPS = 1e-6

kernel_name = 'hybrid_streaming_sb_rglru_ssd_hmoe'


def rmsnorm(x, g):
    x32 = x.astype(jnp.float32)
    y = x32 * lax.rsqrt(jnp.mean(x32 * x32, axis=-1, keepdims=True) + EPS)
    return (y * g.astype(jnp.float32)).astype(x.dtype)


def causal_conv(x, prev, w, b):
    T = x.shape[1]
    xp = jnp.concatenate([prev.astype(x.dtype), x], axis=1)
    y = sum(xp[:, k:k + T] * w[k] for k in range(CONV_W)) + b
    return y, xp[:, -(CONV_W - 1):]


def stick_breaking(q, k, v, q_pos, k_pos):
    z = jnp.einsum('bqhd,bkhd->bhqk', q, k, preferred_element_type=jnp.float32) * (SB_HEAD_DIM ** -0.5)
    mask = k_pos[None, :] < q_pos[:, None]
    log_beta = jax.nn.log_sigmoid(z)
    log_keep = jnp.where(mask, jax.nn.log_sigmoid(-z), 0.0)
    log_after = lax.cumsum(log_keep, axis=3, reverse=True) - log_keep
    w = jnp.where(mask, jnp.exp(log_beta + log_after), 0.0)
    o = jnp.einsum('bhqk,bkhd->bqhd', w, v.astype(jnp.float32))
    return o.astype(q.dtype)


def stick_breaking_prompt(q, k, v):
    B, S, H, Dh = q.shape
    k_pos = jnp.arange(S)

    def block(i):
        start = i * SB_QBLOCK
        qb = lax.dynamic_slice_in_dim(q, start, SB_QBLOCK, axis=1)
        return stick_breaking(qb, k, v, start + jnp.arange(SB_QBLOCK), k_pos)

    o = lax.map(block, jnp.arange(S // SB_QBLOCK))
    return jnp.moveaxis(o, 0, 1).reshape(B, S, H, Dh)


def linear_recurrence(a, b, h0):
    b = b.at[:, 0].add(a[:, 0] * h0)

    def combine(left, right):
        return left[0] * right[0], right[0] * left[1] + right[1]

    _, h = lax.associative_scan(combine, (a, b), axis=1)
    return h


def rglru_branch(x, gate, conv_prev, h0, conv_w, conv_b, w_a, b_a, w_x, b_x, lam):
    B, T, _ = x.shape
    xc, conv_new = causal_conv(x, conv_prev, conv_w, conv_b)
    xc32 = xc.astype(jnp.float32)
    xh = xc32.reshape(B, T, LRU_BLOCKS, LRU_BLOCK_DIM)
    r = jax.nn.sigmoid(jnp.einsum('bthi,hij->bthj', xh, w_a.astype(jnp.float32)).reshape(B, T, LRU_WIDTH) + b_a)
    i = jax.nn.sigmoid(jnp.einsum('bthi,hij->bthj', xh, w_x.astype(jnp.float32)).reshape(B, T, LRU_WIDTH) + b_x)
    log_a = LRU_C * r * jax.nn.log_sigmoid(lam.astype(jnp.float32))
    a = jnp.exp(log_a)
    mult = jnp.sqrt(-jnp.expm1(2.0 * log_a))
    h = linear_recurrence(a, mult * (i * xc32), h0.astype(jnp.float32))
    y = h * jax.nn.gelu(gate.astype(jnp.float32), approximate=True)
    return y.astype(x.dtype), conv_new, h[:, -1]


def ssd_scan(x, dA, Bm, Cm, h0, chunk):
    b, T = x.shape[:2]
    nc = T // chunk
    x = x.reshape(b, nc, chunk, SSM_GROUPS, SSM_HPG, SSM_HEAD_DIM)
    dA = dA.reshape(b, nc, chunk, SSM_GROUPS, SSM_HPG)
    Bm = Bm.reshape(b, nc, chunk, SSM_GROUPS, SSM_STATE)
    Cm = Cm.reshape(b, nc, chunk, SSM_GROUPS, SSM_STATE)
    a_cum = jnp.cumsum(dA, axis=2)
    causal = jnp.tril(jnp.ones((chunk, chunk), bool))[None, None, :, :, None, None]
    seg = a_cum[:, :, :, None] - a_cum[:, :, None, :]
    decay_ls = jnp.exp(jnp.where(causal, seg, -jnp.inf))
    cb = jnp.einsum('bclgn,bcsgn->bclsg', Cm, Bm)
    y_diag = jnp.einsum('bclsg,bclsge,bcsgep->bclgep', cb, decay_ls, x)
    decay_to_end = jnp.exp(a_cum[:, :, -1:] - a_cum)
    chunk_states = jnp.einsum('bclgn,bclge,bclgep->bcgepn', Bm, decay_to_end, x)
    chunk_decay = jnp.exp(a_cum[:, :, -1])

    def step(h, inp):
        d, s = inp
        return d[..., None, None] * h + s, h

    h_last, h_in = lax.scan(step, h0, (jnp.moveaxis(chunk_decay, 1, 0), jnp.moveaxis(chunk_states, 1, 0)))
    h_in = jnp.moveaxis(h_in, 0, 1)
    y_off = jnp.einsum('bclgn,bcgepn,bclge->bclgep', Cm, h_in, jnp.exp(a_cum))
    y = (y_diag + y_off).reshape(b, T, SSM_GROUPS, SSM_HPG, SSM_HEAD_DIM)
    return y, h_last


def ssd_branch(z, xbc, dt_raw, conv_prev, h0, conv_w, conv_b, dt_bias, a_log, d_skip, g_norm):
    B, T, _ = z.shape
    xbc_c, conv_new = causal_conv(xbc, conv_prev, conv_w, conv_b)
    xbc_c = jax.nn.silu(xbc_c.astype(jnp.float32))
    xs, Bm, Cm = jnp.split(xbc_c, [SSM_INNER, SSM_INNER + SSM_GROUPS * SSM_STATE], axis=-1)
    xs = xs.reshape(B, T, SSM_GROUPS, SSM_HPG, SSM_HEAD_DIM)
    Bm = Bm.reshape(B, T, SSM_GROUPS, SSM_STATE)
    Cm = Cm.reshape(B, T, SSM_GROUPS, SSM_STATE)
    dt = jax.nn.softplus(dt_raw.astype(jnp.float32) + dt_bias).reshape(B, T, SSM_GROUPS, SSM_HPG)
    A = -jnp.exp(a_log.astype(jnp.float32)).reshape(SSM_GROUPS, SSM_HPG)
    chunk = CHUNK if T % CHUNK == 0 else T
    h0 = h0.astype(jnp.float32).reshape(B, SSM_GROUPS, SSM_HPG, SSM_HEAD_DIM, SSM_STATE)
    y, h_last = ssd_scan(xs * dt[..., None], dt * A, Bm, Cm, h0, chunk)
    y = y + d_skip.astype(jnp.float32).reshape(SSM_GROUPS, SSM_HPG, 1) * xs
    y = y.reshape(B, T, SSM_GROUPS, SSM_HPG * SSM_HEAD_DIM) * jax.nn.silu(
        z.astype(jnp.float32).reshape(B, T, SSM_GROUPS, SSM_HPG * SSM_HEAD_DIM))
    y = y * lax.rsqrt(jnp.mean(y * y, axis=-1, keepdims=True) + EPS) * g_norm.astype(jnp.float32).reshape(SSM_GROUPS, -1)
    return (y.reshape(B, T, SSM_INNER).astype(z.dtype), conv_new,
            h_last.reshape(B, SSM_HEADS, SSM_HEAD_DIM, SSM_STATE))


def hier_moe(u, w_rg, b_rg, w_re, b_re, w_eg, w_eu, w_ed):
    lead = u.shape[:-1]
    uf = u.reshape(-1, D_MODEL)
    T = uf.shape[0]
    g_logits = (uf @ w_rg + b_rg).astype(jnp.float32)
    p_group = jax.nn.softmax(g_logits, axis=-1)
    g_sel = jnp.argmax(g_logits, axis=-1)
    e_logits = (uf @ w_re + b_re).astype(jnp.float32).reshape(T, N_EXPERT_GROUPS, EXPERTS_PER_GROUP)
    e_in = jnp.take_along_axis(e_logits, g_sel[:, None, None], axis=1)[:, 0]
    top_v, top_i = lax.top_k(e_in, TOP_K)
    gate = jax.nn.softmax(top_v, axis=-1) * jnp.take_along_axis(p_group, g_sel[:, None], axis=1)
    e_flat = (g_sel[:, None] * EXPERTS_PER_GROUP + top_i).reshape(-1)
    w_flat = gate.reshape(-1)
    tok_flat = jnp.repeat(jnp.arange(T), TOP_K)
    order = jnp.argsort(e_flat)
    e_s, tok_s, w_s = e_flat[order], tok_flat[order], w_flat[order]
    counts = jnp.bincount(e_flat, length=N_EXPERTS)
    starts = jnp.cumsum(counts) - counts
    padded = ((counts + MOE_BLOCK - 1) // MOE_BLOCK) * MOE_BLOCK
    pad_ends = jnp.cumsum(padded)
    pad_starts = pad_ends - padded
    n_assign = T * TOP_K
    dest = pad_starts[e_s] + (jnp.arange(n_assign) - starts[e_s])
    n_blocks = -(-n_assign // MOE_BLOCK) + N_EXPERTS
    cap = n_blocks * MOE_BLOCK
    x_buf = jnp.zeros((cap, D_MODEL), uf.dtype).at[dest].set(uf[tok_s])
    w_buf = jnp.zeros((cap,), jnp.float32).at[dest].set(w_s)
    t_buf = jnp.zeros((cap,), jnp.int32).at[dest].set(tok_s.astype(jnp.int32))
    blk_expert = jnp.minimum(jnp.searchsorted(pad_ends, jnp.arange(n_blocks) * MOE_BLOCK, side='right'),
                             N_EXPERTS - 1)

    def run_block(args):
        xb, e = args
        return (jax.nn.silu(xb @ w_eg[e]) * (xb @ w_eu[e])) @ w_ed[e]

    y_buf = lax.map(run_block, (x_buf.reshape(n_blocks, MOE_BLOCK, D_MODEL), blk_expert)).reshape(cap, D_MODEL)
    out = jnp.zeros((T, D_MODEL), jnp.float32).at[t_buf].add(y_buf.astype(jnp.float32) * w_buf[:, None])
    return out.reshape(*lead, D_MODEL).astype(u.dtype)


def token_mixers(u, lp, past):
    B, T, _ = u.shape
    proj = u @ lp['w_in']
    q, k, v, xb, gb, zc, xbc, dtr = jnp.split(proj, IN_SPLITS, axis=-1)
    head_shape = (B, T, SB_HEADS, SB_HEAD_DIM)
    q, k, v = q.reshape(head_shape), k.reshape(head_shape), v.reshape(head_shape)
    if past is None:
        o_a = stick_breaking_prompt(q, k, v)
        lru_conv0 = jnp.zeros((B, CONV_W - 1, LRU_WIDTH), u.dtype)
        lru_h0 = jnp.zeros((B, LRU_WIDTH), jnp.float32)
        ssm_conv0 = jnp.zeros((B, CONV_W - 1, SSM_CONV_DIM), u.dtype)
        ssm_h0 = jnp.zeros((B, SSM_HEADS, SSM_HEAD_DIM, SSM_STATE), jnp.float32)
    else:
        k_past, v_past, lru_conv0, lru_h0, ssm_conv0, ssm_h0 = past
        P = k_past.shape[1]
        k_all = jnp.concatenate([k_past.astype(k.dtype), k], axis=1)
        v_all = jnp.concatenate([v_past.astype(v.dtype), v], axis=1)
        o_a = stick_breaking(q, k_all, v_all, P + jnp.arange(T), jnp.arange(P + T))
    o_a = o_a.reshape(B, T, SB_WIDTH)
    o_b, lru_conv1, lru_h1 = rglru_branch(xb, gb, lru_conv0, lru_h0, lp['lru_conv_w'], lp['lru_conv_b'],
                                          lp['lru_w_a'], lp['lru_b_a'], lp['lru_w_x'], lp['lru_b_x'],
                                          lp['lru_lambda'])
    o_c, ssm_conv1, ssm_h1 = ssd_branch(zc, xbc, dtr, ssm_conv0, ssm_h0, lp['ssm_conv_w'], lp['ssm_conv_b'],
                                        lp['ssm_dt_bias'], lp['ssm_a_log'], lp['ssm_d'], lp['ssm_norm_g'])
    gates = jax.nn.sigmoid(u @ lp['w_gate'] + lp['b_gate']).reshape(B, T, N_BRANCH, D_MODEL)
    wbo = lp['w_branch_out']
    merged = (gates[:, :, 0] * (o_a @ wbo[:SB_WIDTH])
              + gates[:, :, 1] * (o_b @ wbo[SB_WIDTH:SB_WIDTH + LRU_WIDTH])
              + gates[:, :, 2] * (o_c @ wbo[SB_WIDTH + LRU_WIDTH:]))
    return merged @ lp['w_out'], (k, v, lru_conv1, lru_h1, ssm_conv1, ssm_h1)


def run_trunk(x, layer_params, g_final, past):
    h = x
    per_layer = []
    for l in range(DEPTH):
        lp = {name: arr[l] for name, arr in layer_params.items()}
        layer_past = None if past is None else tuple(p[l] for p in past)
        mix, st = token_mixers(rmsnorm(h, lp['g_mix']), lp, layer_past)
        h = h + mix
        h = h + hier_moe(rmsnorm(h, lp['g_ffn']), lp['w_router_group'], lp['b_router_group'],
                         lp['w_router_expert'], lp['b_router_expert'],
                         lp['w_expert_gate'], lp['w_expert_up'], lp['w_expert_down'])
        per_layer.append(st)
    new_states = tuple(jnp.stack([s[i] for s in per_layer], axis=0) for i in range(6))
    return rmsnorm(h, g_final), new_states


def setup_inputs(seed: int = 0) -> dict:
    key = jax.random.key(seed)
    keys = iter(jax.random.split(key, 48))

    def nrm(shape, scale):
        return jax.random.normal(next(keys), shape, jnp.float32) * scale

    L = DEPTH
    x_prompt = nrm((BATCH, SEQ, D_MODEL), 1.0)
    x_sample = nrm((DEC_BATCH, DEC_SEQ, D_MODEL), 1.0)
    cache_sb_k = nrm((L, DEC_BATCH, PAST_LEN, SB_HEADS, SB_HEAD_DIM), 1.0)
    cache_sb_v = nrm((L, DEC_BATCH, PAST_LEN, SB_HEADS, SB_HEAD_DIM), 1.0)
    state_lru_conv = nrm((L, DEC_BATCH, CONV_W - 1, LRU_WIDTH), 1.0)
    state_lru_h = nrm((L, DEC_BATCH, LRU_WIDTH), 0.5)
    state_ssm_conv = nrm((L, DEC_BATCH, CONV_W - 1, SSM_CONV_DIM), 1.0)
    state_ssm_h = nrm((L, DEC_BATCH, SSM_HEADS, SSM_HEAD_DIM, SSM_STATE), 0.1)
    g_mix = 1.0 + nrm((L, D_MODEL), 0.02)
    w_in = nrm((L, D_MODEL, IN_PROJ_WIDTH), D_MODEL ** -0.5)
    w_gate = nrm((L, D_MODEL, N_BRANCH * D_MODEL), D_MODEL ** -0.5)
    b_gate = nrm((L, N_BRANCH * D_MODEL), 0.02)
    w_branch_out = nrm((L, MIX_WIDTH, D_MODEL), (MIX_WIDTH // N_BRANCH) ** -0.5)
    w_out = nrm((L, D_MODEL, D_MODEL), D_MODEL ** -0.5)
    lru_conv_w = nrm((L, CONV_W, LRU_WIDTH), CONV_W ** -0.5)
    lru_conv_b = nrm((L, LRU_WIDTH), 0.02)
    lru_w_a = nrm((L, LRU_BLOCKS, LRU_BLOCK_DIM, LRU_BLOCK_DIM), LRU_BLOCK_DIM ** -0.5)
    lru_b_a = nrm((L, LRU_WIDTH), 0.02)
    lru_w_x = nrm((L, LRU_BLOCKS, LRU_BLOCK_DIM, LRU_BLOCK_DIM), LRU_BLOCK_DIM ** -0.5)
    lru_b_x = nrm((L, LRU_WIDTH), 0.02)
    a0 = jax.random.uniform(next(keys), (L, LRU_WIDTH), jnp.float32, minval=0.9, maxval=0.999)
    lru_lambda = jnp.log(a0) - jnp.log1p(-a0)
    ssm_conv_w = nrm((L, CONV_W, SSM_CONV_DIM), CONV_W ** -0.5)
    ssm_conv_b = nrm((L, SSM_CONV_DIM), 0.02)
    dt0 = jnp.exp(jax.random.uniform(next(keys), (L, SSM_HEADS), jnp.float32,
                                     minval=math.log(1e-3), maxval=math.log(1e-1)))
    ssm_dt_bias = dt0 + jnp.log(-jnp.expm1(-dt0))
    ssm_a_log = jnp.log(jax.random.uniform(next(keys), (L, SSM_HEADS), jnp.float32, minval=1.0, maxval=16.0))
    ssm_d = 1.0 + nrm((L, SSM_HEADS), 0.1)
    ssm_norm_g = 1.0 + nrm((L, SSM_INNER), 0.02)
    g_ffn = 1.0 + nrm((L, D_MODEL), 0.02)
    w_router_group = nrm((L, D_MODEL, N_EXPERT_GROUPS), D_MODEL ** -0.5)
    b_router_group = nrm((L, N_EXPERT_GROUPS), 0.01)
    w_router_expert = nrm((L, D_MODEL, N_EXPERTS), D_MODEL ** -0.5)
    b_router_expert = nrm((L, N_EXPERTS), 0.01)
    w_expert_gate = nrm((L, N_EXPERTS, D_MODEL, D_EXPERT), D_MODEL ** -0.5)
    w_expert_up = nrm((L, N_EXPERTS, D_MODEL, D_EXPERT), D_MODEL ** -0.5)
    w_expert_down = nrm((L, N_EXPERTS, D_EXPERT, D_MODEL), D_EXPERT ** -0.5)
    g_final = 1.0 + nrm((D_MODEL,), 0.02)
    return {'x_prompt': x_prompt, 'x_sample': x_sample,
            'cache_sb_k': cache_sb_k, 'cache_sb_v': cache_sb_v,
            'state_lru_conv': state_lru_conv, 'state_lru_h': state_lru_h,
            'state_ssm_conv': state_ssm_conv, 'state_ssm_h': state_ssm_h,
            'g_mix': g_mix, 'w_in': w_in, 'w_gate': w_gate, 'b_gate': b_gate,
            'w_branch_out': w_branch_out, 'w_out': w_out,
            'lru_conv_w': lru_conv_w, 'lru_conv_b': lru_conv_b, 'lru_w_a': lru_w_a, 'lru_b_a': lru_b_a,
            'lru_w_x': lru_w_x, 'lru_b_x': lru_b_x, 'lru_lambda': lru_lambda,
            'ssm_conv_w': ssm_conv_w, 'ssm_conv_b': ssm_conv_b, 'ssm_dt_bias': ssm_dt_bias,
            'ssm_a_log': ssm_a_log, 'ssm_d': ssm_d, 'ssm_norm_g': ssm_norm_g,
            'g_ffn': g_ffn, 'w_router_group': w_router_group, 'b_router_group': b_router_group,
            'w_router_expert': w_router_expert, 'b_router_expert': b_router_expert,
            'w_expert_gate': w_expert_gate, 'w_expert_up': w_expert_up, 'w_expert_down': w_expert_down,
            'g_final': g_final}


def reference(x_prompt, x_sample, cache_sb_k, cache_sb_v, state_lru_conv, state_lru_h, state_ssm_conv,
              state_ssm_h, g_mix, w_in, w_gate, b_gate, w_branch_out, w_out, lru_conv_w, lru_conv_b,
              lru_w_a, lru_b_a, lru_w_x, lru_b_x, lru_lambda, ssm_conv_w, ssm_conv_b, ssm_dt_bias,
              ssm_a_log, ssm_d, ssm_norm_g, g_ffn, w_router_group, b_router_group, w_router_expert,
              b_router_expert, w_expert_gate, w_expert_up, w_expert_down, g_final):
    layer_params = {'g_mix': g_mix, 'w_in': w_in, 'w_gate': w_gate, 'b_gate': b_gate,
                    'w_branch_out': w_branch_out, 'w_out': w_out,
                    'lru_conv_w': lru_conv_w, 'lru_conv_b': lru_conv_b, 'lru_w_a': lru_w_a, 'lru_b_a': lru_b_a,
                    'lru_w_x': lru_w_x, 'lru_b_x': lru_b_x, 'lru_lambda': lru_lambda,
                    'ssm_conv_w': ssm_conv_w, 'ssm_conv_b': ssm_conv_b, 'ssm_dt_bias': ssm_dt_bias,
                    'ssm_a_log': ssm_a_log, 'ssm_d': ssm_d, 'ssm_norm_g': ssm_norm_g,
                    'g_ffn': g_ffn, 'w_router_group': w_router_group, 'b_router_group': b_router_group,
                    'w_router_expert': w_router_expert, 'b_router_expert': b_router_expert,
                    'w_expert_gate': w_expert_gate, 'w_expert_up': w_expert_up, 'w_expert_down': w_expert_down}
    y_prompt, (kp, vp, lcp, lhp, scp, shp) = run_trunk(x_prompt, layer_params, g_final, None)
    y_sample, (ks, vs, lcs, lhs, scs, shs) = run_trunk(
        x_sample, layer_params, g_final,
        (cache_sb_k, cache_sb_v, state_lru_conv, state_lru_h, state_ssm_conv, state_ssm_h))
    return (y_prompt, y_sample, kp, vp, lcp, lhp, scp, shp, ks, vs, lcs, lhs, scs, shs)
```

```python
import functools

import jax
import jax.numpy as jnp
from jax import lax
from jax.experimental import pallas as pl
from jax.experimental.pallas import tpu as pltpu

F32 = jnp.float32
BF16 = jnp.bfloat16
I32 = jnp.int32

D_MODEL = 1024
N_HEADS = 8
HEAD_DIM = 64
WIDTH = 512
CONV_W = 4
SSM_STATE = 128
SSM_CONV_DIM = 1024
N_GROUPS = 4
PER_GROUP = 8
N_EXPERTS = 32
D_EXPERT = 512
LRU_C = 8.0
EPS = 1e-6

LANES = 128
SUBLANES = 8
TOKEN_TILE = 512
PROJ_TILE = 256
KEY_BLOCK = 128
SSD_CHUNK = 64
SAMPLE_WINDOW = 512
EXPERT_BLOCK = 256
VMEM_LIMIT = 56 * 1024 * 1024

LOG_CUTOFF = -88.0

REST_XBC, REST_XB, REST_GB, REST_ZC, REST_DT = 0, 1024, 1536, 2048, 2560
REST_W = 2688
QM_W = 1024
PROJ_W = QM_W + 2 * WIDTH + REST_W


def _cparams(sem):
    return pltpu.CompilerParams(dimension_semantics=sem, vmem_limit_bytes=VMEM_LIMIT)


def _split3(x):
    hi = x.astype(BF16)
    r = x - hi.astype(F32)
    mid = r.astype(BF16)
    lo = (r - mid.astype(F32)).astype(BF16)
    return hi, mid, lo


def _dot(a, b):
    return jnp.dot(a, b, preferred_element_type=F32)


def _dot_nt(a, b):
    return lax.dot_general(a, b, (((1,), (1,)), ((), ())), preferred_element_type=F32)


def _dot_tn(a, b):
    return lax.dot_general(a, b, (((0,), (0,)), ((), ())), preferred_element_type=F32)


def _sel_right(x, m01):
    return sum(_dot(p, m01) for p in _split3(x))


def _sel_left(m01, x):
    return sum(_dot(m01, p) for p in _split3(x))


def _rmsnorm(x, g):
    return x * lax.rsqrt(jnp.mean(x * x, axis=-1, keepdims=True) + EPS) * g


def _log_sigmoid(z):
    return jnp.minimum(z, 0.0) - jnp.log1p(jnp.exp(-jnp.abs(z)))


def _silu(x):
    return x * jax.nn.sigmoid(x)


def _iota(shape, dim):
    return lax.broadcasted_iota(I32, shape, dim)


def _inproj_body(h_ref, g_ref, w_ref, qm_ref, k_ref, v_ref, kb_ref, vb_ref, rest_ref):
    ub = _rmsnorm(h_ref[...], g_ref[...]).astype(BF16)

    def proj(c0, c1):
        return _dot(ub, w_ref[:, c0:c1])

    for c in range(0, QM_W, 512):
        qm_ref[:, c:c + 512] = proj(c, c + 512).astype(BF16)
    k = proj(QM_W, QM_W + WIDTH)
    k_ref[...] = k
    kb_ref[...] = k.astype(BF16)
    v = proj(QM_W + WIDTH, QM_W + 2 * WIDTH)
    v_ref[...] = v
    vb_ref[...] = v.astype(BF16)
    base = QM_W + 2 * WIDTH
    for c in range(0, REST_W, 512):
        c1 = min(c + 512, REST_W)
        rest_ref[:, c:c1] = proj(base + c, base + c1)


def _inproj(h, g, w):
    t = h.shape[0]
    tm = PROJ_TILE
    row = lambda i: (i, 0)
    fixed = lambda i: (0, 0)
    return pl.pallas_call(
        _inproj_body,
        grid=(t // tm,),
        in_specs=[pl.BlockSpec((tm, D_MODEL), row), pl.BlockSpec((1, D_MODEL), fixed),
                  pl.BlockSpec((D_MODEL, PROJ_W), fixed)],
        out_specs=[pl.BlockSpec((tm, QM_W), row), pl.BlockSpec((tm, WIDTH), row),
                   pl.BlockSpec((tm, WIDTH), row), pl.BlockSpec((tm, WIDTH), row),
                   pl.BlockSpec((tm, WIDTH), row), pl.BlockSpec((tm, REST_W), row)],
        out_shape=[jax.ShapeDtypeStruct((t, QM_W), BF16), jax.ShapeDtypeStruct((t, WIDTH), F32),
                   jax.ShapeDtypeStruct((t, WIDTH), F32), jax.ShapeDtypeStruct((t, WIDTH), BF16),
                   jax.ShapeDtypeStruct((t, WIDTH), BF16), jax.ShapeDtypeStruct((t, REST_W), F32)],
        compiler_params=_cparams(("arbitrary",)),
        name="inproj",
    )(h, g, w)


def _sb_block(qm, kblk, vblk, mask, acc_ref, r_ref, tri_ref):
    tq, kb = qm.shape[0], kblk.shape[0]
    low = _iota((tq, LANES), 1) < HEAD_DIM
    rmax = None
    for p in range(N_HEADS // 2):
        k2 = kblk[:, LANES * p:LANES * (p + 1)]
        v2 = vblk[:, LANES * p:LANES * (p + 1)]
        o_pair = None
        for s in range(2):
            h = 2 * p + s
            z = _dot_nt(qm[:, LANES * h:LANES * (h + 1)], k2) * (HEAD_DIM ** -0.5)
            ls = _log_sigmoid(z)
            lk = ls - z
            if mask is not None:
                lk = jnp.where(mask, lk, 0.0)
            parts = _split3(lk)
            if kb == KEY_BLOCK:
                res = _dot(jnp.concatenate(parts, axis=1), tri_ref[...])
                la, tot = res[:, :KEY_BLOCK], res[:, KEY_BLOCK:]
            else:
                la = _sel_right(lk, tri_ref[0:kb, 0:kb])
                tot = _sel_right(lk, tri_ref[0:kb, KEY_BLOCK:2 * KEY_BLOCK])
            r_old = r_ref[h]
            w = jnp.exp(ls + la + r_old[:, :kb])
            if mask is not None:
                w = jnp.where(mask, w, 0.0)
            o = _dot(w.astype(BF16), v2)
            o = jnp.where(low if s == 0 else jnp.logical_not(low), o, 0.0)
            o_pair = o if s == 0 else o_pair + o
            r_new = r_old + tot
            r_ref[h] = r_new
            m = jnp.max(r_new)
            rmax = m if rmax is None else jnp.maximum(rmax, m)
        acc_ref[:, LANES * p:LANES * (p + 1)] += o_pair
    return rmax


def _more_keys(c):
    j, rmax = c
    return jnp.logical_and(j >= 0, rmax > LOG_CUTOFF)


def _attn_prompt_body(qm_ref, kb_ref, vb_ref, tri_ref, o_ref, acc_ref, r_ref):
    i = pl.program_id(1)
    tq = qm_ref.shape[0]
    acc_ref[...] = jnp.zeros_like(acc_ref)
    r_ref[...] = jnp.zeros_like(r_ref)
    qm = qm_ref[...]
    qpos = _iota((tq, KEY_BLOCK), 0) + i * tq
    col = _iota((tq, KEY_BLOCK), 1)

    def body(c):
        j, _ = c
        off = pl.multiple_of(j * KEY_BLOCK, KEY_BLOCK)
        mask = col + j * KEY_BLOCK < qpos
        rmax = _sb_block(qm, kb_ref[pl.ds(off, KEY_BLOCK), :], vb_ref[pl.ds(off, KEY_BLOCK), :],
                         mask, acc_ref, r_ref, tri_ref)
        return j - 1, rmax

    lax.while_loop(_more_keys, body, (i, jnp.float32(0.0)))
    o_ref[...] = acc_ref[...].astype(BF16)


def _attn_prompt(qm, kb, vb, tri, n_seq, seq_len):
    tq = KEY_BLOCK
    nq = seq_len // tq
    return pl.pallas_call(
        _attn_prompt_body,
        grid=(n_seq, nq),
        in_specs=[pl.BlockSpec((tq, QM_W), lambda b, i: (b * nq + i, 0)),
                  pl.BlockSpec((seq_len, WIDTH), lambda b, i: (b, 0)),
                  pl.BlockSpec((seq_len, WIDTH), lambda b, i: (b, 0)),
                  pl.BlockSpec((3 * KEY_BLOCK, 2 * KEY_BLOCK), lambda b, i: (0, 0))],
        out_specs=pl.BlockSpec((tq, WIDTH), lambda b, i: (b * nq + i, 0)),
        out_shape=jax.ShapeDtypeStruct((n_seq * seq_len, WIDTH), BF16),
        scratch_shapes=[pltpu.VMEM((tq, WIDTH), F32), pltpu.VMEM((N_HEADS, tq, LANES), F32)],
        compiler_params=_cparams(("arbitrary", "arbitrary")),
        name="attn_prompt",
    )(qm, kb, vb, tri)


def _pack_r(r_ref, tq):
    lane = _iota((tq, LANES), 1)
    rp = jnp.zeros((tq, LANES), F32)
    for h in range(N_HEADS):
        rp = jnp.where(lane == h, r_ref[h], rp)
    return rp


def _attn_sample_body(qm_ref, kn_ref, vn_ref, ck_ref, cv_ref, tri_ref, acc_out, rp_out, acc_ref, r_ref):
    tq = qm_ref.shape[0]
    acc_ref[...] = jnp.zeros_like(acc_ref)
    r_ref[...] = jnp.zeros_like(r_ref)
    qm = qm_ref[...]
    causal = _iota((tq, tq), 1) < _iota((tq, tq), 0)
    rmax = _sb_block(qm, kn_ref[...], vn_ref[...], causal, acc_ref, r_ref, tri_ref)

    def body(c):
        j, _ = c
        off = pl.multiple_of(j * KEY_BLOCK, KEY_BLOCK)
        kblk = ck_ref[0, 0, pl.ds(off, KEY_BLOCK), :].astype(BF16)
        vblk = cv_ref[0, 0, pl.ds(off, KEY_BLOCK), :].astype(BF16)
        return j - 1, _sb_block(qm, kblk, vblk, None, acc_ref, r_ref, tri_ref)

    lax.while_loop(_more_keys, body, (jnp.int32(ck_ref.shape[2] // KEY_BLOCK - 1), rmax))
    acc_out[...] = acc_ref[...]
    rp_out[...] = _pack_r(r_ref, tq)


def _attn_sample(qm, kb, vb, cache_k, cache_v, tri, layer, row0, n_seq, tq):
    past = cache_k.shape[2]
    win = min(SAMPLE_WINDOW, past)
    blk0 = row0 // tq
    cur = lambda b: (blk0 + b, 0)
    cache = lambda b: (layer, b, past // win - 1, 0)
    return pl.pallas_call(
        _attn_sample_body,
        grid=(n_seq,),
        in_specs=[pl.BlockSpec((tq, QM_W), cur), pl.BlockSpec((tq, WIDTH), cur), pl.BlockSpec((tq, WIDTH), cur),
                  pl.BlockSpec((1, 1, win, WIDTH), cache), pl.BlockSpec((1, 1, win, WIDTH), cache),
                  pl.BlockSpec((3 * KEY_BLOCK, 2 * KEY_BLOCK), lambda b: (0, 0))],
        out_specs=[pl.BlockSpec((tq, WIDTH), lambda b: (b, 0)), pl.BlockSpec((tq, LANES), lambda b: (b, 0))],
        out_shape=[jax.ShapeDtypeStruct((n_seq * tq, WIDTH), F32), jax.ShapeDtypeStruct((n_seq * tq, LANES), F32)],
        scratch_shapes=[pltpu.VMEM((tq, WIDTH), F32), pltpu.VMEM((N_HEADS, tq, LANES), F32)],
        compiler_params=_cparams(("arbitrary",)),
        name="attn_sample",
    )(qm, kb, vb, cache_k, cache_v, tri)


def _attn_older_body(qm_ref, ck_ref, cv_ref, acc_in, rp_in, tri_ref, acc_out, acc_ref, r_ref):
    s = pl.program_id(1)
    tq = qm_ref.shape[0]

    @pl.when(s == 0)
    def _():
        acc_ref[...] = acc_in[...]
        rp = rp_in[...]
        for h in range(N_HEADS):
            r_ref[h] = jnp.broadcast_to(rp[:, h:h + 1], (tq, LANES))

    rmax = jnp.max(r_ref[0])
    for h in range(1, N_HEADS):
        rmax = jnp.maximum(rmax, jnp.max(r_ref[h]))

    @pl.when(rmax > LOG_CUTOFF)
    def _():
        _sb_block(qm_ref[...], ck_ref[0, 0].astype(BF16), cv_ref[0, 0].astype(BF16), None, acc_ref, r_ref, tri_ref)

    @pl.when(s == pl.num_programs(1) - 1)
    def _():
        acc_out[...] = acc_ref[...]


def _attn_older(qm, cache_k, cache_v, acc, rp, tri, layer, row0, n_seq, tq):
    past = cache_k.shape[2]
    win = min(SAMPLE_WINDOW, past)
    nb = (past - win) // KEY_BLOCK
    blk0 = row0 // tq
    cache = lambda b, s: (layer, b, nb - 1 - s, 0)
    return pl.pallas_call(
        _attn_older_body,
        grid=(n_seq, nb),
        in_specs=[pl.BlockSpec((tq, QM_W), lambda b, s: (blk0 + b, 0)),
                  pl.BlockSpec((1, 1, KEY_BLOCK, WIDTH), cache), pl.BlockSpec((1, 1, KEY_BLOCK, WIDTH), cache),
                  pl.BlockSpec((tq, WIDTH), lambda b, s: (b, 0)), pl.BlockSpec((tq, LANES), lambda b, s: (b, 0)),
                  pl.BlockSpec((3 * KEY_BLOCK, 2 * KEY_BLOCK), lambda b, s: (0, 0))],
        out_specs=pl.BlockSpec((tq, WIDTH), lambda b, s: (b, 0)),
        out_shape=jax.ShapeDtypeStruct((n_seq * tq, WIDTH), F32),
        scratch_shapes=[pltpu.VMEM((tq, WIDTH), F32), pltpu.VMEM((N_HEADS, tq, LANES), F32)],
        compiler_params=_cparams(("arbitrary", "arbitrary")),
        name="attn_older",
    )(qm, cache_k, cache_v, acc, rp, tri)


def _causal_conv(xp_ref, n, w_ref, b_ref):
    y = b_ref[...]
    for k in range(CONV_W):
        y = y + xp_ref[pl.ds(SUBLANES - (CONV_W - 1) + k, n), :] * w_ref[k:k + 1, :]
    return y


def _lru_rows(xp_ref, a_ref, b_ref, n, h0, gate, cw_ref, cb_ref, wa_ref, ba_ref, wx_ref, bx_ref, lam_ref):
    xc = _causal_conv(xp_ref, n, cw_ref, cb_ref)
    xcb = xc.astype(BF16)
    r = jax.nn.sigmoid(_dot(xcb, wa_ref[...]) + ba_ref[...])
    ig = jax.nn.sigmoid(_dot(xcb, wx_ref[...]) + bx_ref[...])
    log_a = LRU_C * r * _log_sigmoid(lam_ref[...])
    a_ref[0:n, :] = jnp.exp(log_a)
    th = jnp.tanh(log_a)
    b_ref[0:n, :] = jnp.sqrt(-2.0 * th / (1.0 - th)) * (ig * xc)
    row = _iota((SUBLANES, WIDTH), 0)

    def group(g, h):
        off = pl.multiple_of(g * SUBLANES, SUBLANES)
        a = a_ref[pl.ds(off, SUBLANES), :]
        b = b_ref[pl.ds(off, SUBLANES), :]
        for s in (1, 2, 4):
            a_prev = jnp.where(row >= s, pltpu.roll(a, s, 0), 1.0)
            b_prev = jnp.where(row >= s, pltpu.roll(b, s, 0), 0.0)
            b = b + a * b_prev
            a = a * a_prev
        hs = a * h + b
        b_ref[pl.ds(off, SUBLANES), :] = hs
        return hs[SUBLANES - 1:SUBLANES, :]

    h_last = lax.fori_loop(0, n // SUBLANES, group, h0)
    y = b_ref[0:n, :] * jax.nn.gelu(gate, approximate=True)
    return y, h_last


def _lru_body(n_ptiles, tps, seg, xb_ref, gb_ref, conv0_ref, h0_ref, cw_ref, cb_ref, wa_ref, ba_ref, wx_ref,
              bx_ref, lam_ref, o_ref, convp_ref, hp_ref, convs_ref, hs_ref, xp_ref, a_ref, b_ref, hc_ref):
    i = pl.program_id(0)
    tt = xb_ref.shape[0]
    params = (cw_ref, cb_ref, wa_ref, ba_ref, wx_ref, bx_ref, lam_ref)

    @pl.when(i < n_ptiles)
    def _():
        @pl.when(i % tps == 0)
        def _():
            xp_ref[0:SUBLANES, :] = jnp.zeros((SUBLANES, WIDTH), F32)
            hc_ref[...] = jnp.zeros_like(hc_ref)

        xp_ref[SUBLANES:SUBLANES + tt, :] = xb_ref[...]
        y, h_last = _lru_rows(xp_ref, a_ref, b_ref, tt, hc_ref[...], gb_ref[...], *params)
        o_ref[...] = y.astype(BF16)
        hc_ref[...] = h_last
        xp_ref[0:SUBLANES, :] = xp_ref[tt:tt + SUBLANES, :]

        @pl.when(i % tps == tps - 1)
        def _():
            convp_ref[0] = xp_ref[SUBLANES - (CONV_W - 1):SUBLANES, :]
            hp_ref[0] = h_last

    @pl.when(i >= n_ptiles)
    def _():
        for s in range(tt // seg):
            xp_ref[SUBLANES - (CONV_W - 1):SUBLANES, :] = conv0_ref[s]
            xp_ref[SUBLANES:SUBLANES + seg, :] = xb_ref[s * seg:(s + 1) * seg, :]
            y, h_last = _lru_rows(xp_ref, a_ref, b_ref, seg, h0_ref[s:s + 1, :],
                                  gb_ref[s * seg:(s + 1) * seg, :], *params)
            o_ref[s * seg:(s + 1) * seg, :] = y.astype(BF16)
            convs_ref[s] = xp_ref[seg + SUBLANES - (CONV_W - 1):seg + SUBLANES, :]
            hs_ref[s:s + 1, :] = h_last


def _mixer_specs(n_ptiles, tps, spt):
    pseq = lambda i: jnp.minimum(i, n_ptiles - 1) // tps
    stile = lambda i: jnp.maximum(i - n_ptiles, 0)
    return pseq, stile


def _lru(rest, conv0, h0, cw, cb, wa, ba, wx, bx, lam, n_pseq, seq_len, n_sseq, seg):
    t = rest.shape[0]
    tt = TOKEN_TILE
    tps = seq_len // tt
    n_ptiles = n_pseq * tps
    spt = tt // seg
    pseq, stile = _mixer_specs(n_ptiles, tps, spt)
    fixed = lambda i: (0, 0)
    vec = pl.BlockSpec((1, WIDTH), fixed)
    return pl.pallas_call(
        functools.partial(_lru_body, n_ptiles, tps, seg),
        grid=(t // tt,),
        in_specs=[pl.BlockSpec((tt, WIDTH), lambda i: (i, REST_XB // WIDTH)),
                  pl.BlockSpec((tt, WIDTH), lambda i: (i, REST_GB // WIDTH)),
                  pl.BlockSpec((spt, CONV_W - 1, WIDTH), lambda i: (stile(i), 0, 0)),
                  pl.BlockSpec((spt, WIDTH), lambda i: (stile(i), 0)),
                  pl.BlockSpec((CONV_W, WIDTH), fixed), vec,
                  pl.BlockSpec((WIDTH, WIDTH), fixed), vec, pl.BlockSpec((WIDTH, WIDTH), fixed), vec, vec],
        out_specs=[pl.BlockSpec((tt, WIDTH), lambda i: (i, 0)),
                   pl.BlockSpec((1, CONV_W - 1, WIDTH), lambda i: (pseq(i), 0, 0)),
                   pl.BlockSpec((1, 1, WIDTH), lambda i: (pseq(i), 0, 0)),
                   pl.BlockSpec((spt, CONV_W - 1, WIDTH), lambda i: (stile(i), 0, 0)),
                   pl.BlockSpec((spt, WIDTH), lambda i: (stile(i), 0))],
        out_shape=[jax.ShapeDtypeStruct((t, WIDTH), BF16),
                   jax.ShapeDtypeStruct((n_pseq, CONV_W - 1, WIDTH), F32),
                   jax.ShapeDtypeStruct((n_pseq, 1, WIDTH), F32),
                   jax.ShapeDtypeStruct((n_sseq, CONV_W - 1, WIDTH), F32),
                   jax.ShapeDtypeStruct((n_sseq, WIDTH), F32)],
        scratch_shapes=[pltpu.VMEM((tt + SUBLANES, WIDTH), F32), pltpu.VMEM((tt, WIDTH), F32),
                        pltpu.VMEM((tt, WIDTH), F32), pltpu.VMEM((1, WIDTH), F32)],
        compiler_params=_cparams(("arbitrary",)),
        name="rglru",
    )(rest, rest, conv0, h0, cw, cb, wa, ba, wx, bx, lam)


def _ssd_chunk(xc_ref, z_ref, dtr_ref, y_ref, hst_ref, r0, q, dtb_ref, alog_ref, dsk_ref, gn_ref, exp_ref,
               tril_ref, eye_ref):
    xs = xc_ref[r0:r0 + q, 0:WIDTH]
    bm = xc_ref[r0:r0 + q, WIDTH:WIDTH + 2 * SSM_STATE].astype(BF16)
    cm = xc_ref[r0:r0 + q, WIDTH + 2 * SSM_STATE:SSM_CONV_DIM].astype(BF16)
    dt = jax.nn.softplus(dtr_ref[r0:r0 + q, :] + dtb_ref[...])
    da = dt * (-jnp.exp(alog_ref[...]))
    a_cum = _sel_left(tril_ref[0:q, 0:q], da)
    a_exp = _sel_right(a_cum, exp_ref[...])
    dt_exp = _sel_right(dt, exp_ref[...])
    a_cum_t = sum(_dot_nt(eye_ref[...], p) for p in _split3(a_cum))
    a_last = a_cum[q - 1:q, :]
    xdt = xs * dt_exp
    xdtb = xdt.astype(BF16)
    xw = (xdt * jnp.exp(a_exp[q - 1:q, :] - a_exp)).astype(BF16)
    ea = jnp.exp(a_exp)
    causal = _iota((q, q), 0) >= _iota((q, q), 1)
    for g in range(2):
        bg = bm[:, SSM_STATE * g:SSM_STATE * (g + 1)]
        cg = cm[:, SSM_STATE * g:SSM_STATE * (g + 1)]
        cb = _dot_nt(cg, bg)
        for e in range(4 * g, 4 * g + 4):
            hs = slice(HEAD_DIM * e, HEAD_DIM * (e + 1))
            seg = a_cum[:, e:e + 1] - a_cum_t[e:e + 1, :]
            m = (cb * jnp.exp(jnp.where(causal, seg, -1e30))).astype(BF16)
            h_old = hst_ref[e]
            y = _dot(m, xdtb[:, hs]) + _dot_nt(cg, h_old.astype(BF16)) * ea[:, hs]
            y_ref[r0:r0 + q, hs] = y
            decay = jnp.exp(jnp.broadcast_to(a_last[:, e:e + 1], (1, SSM_STATE)))
            hst_ref[e] = decay * h_old + _dot_tn(xw[:, hs], bg)
    y = y_ref[r0:r0 + q, :] + dsk_ref[...] * xs
    y = y * _silu(z_ref[r0:r0 + q, :])
    half = WIDTH // 2
    outs = []
    for g in range(2):
        yg = y[:, half * g:half * (g + 1)]
        outs.append(yg * lax.rsqrt(jnp.mean(yg * yg, axis=-1, keepdims=True) + EPS))
    return jnp.concatenate(outs, axis=1) * gn_ref[...]


def _ssd_body(n_ptiles, tps, seg, q_prompt, xbc_ref, z_ref, dtr_ref, conv0_ref, h0_ref, cw_ref, cb_ref, dtb_ref,
              alog_ref, dsk_ref, gn_ref, exp_ref, tril_ref, eye_ref, o_ref, convp_ref, hp_ref, convs_ref, hs_ref,
              xp_ref, xc_ref, y_ref, hst_ref):
    i = pl.program_id(0)
    tt = xbc_ref.shape[0]
    params = (dtb_ref, alog_ref, dsk_ref, gn_ref, exp_ref, tril_ref, eye_ref)
    tail = slice(SUBLANES - (CONV_W - 1), SUBLANES)

    @pl.when(i < n_ptiles)
    def _():
        @pl.when(i % tps == 0)
        def _():
            xp_ref[0:SUBLANES, :] = jnp.zeros((SUBLANES, SSM_CONV_DIM), F32)
            hst_ref[...] = jnp.zeros_like(hst_ref)

        xp_ref[SUBLANES:SUBLANES + tt, :] = xbc_ref[...]
        xc_ref[...] = _silu(_causal_conv(xp_ref, tt, cw_ref, cb_ref))
        for c in range(tt // q_prompt):
            r0 = c * q_prompt
            o_ref[r0:r0 + q_prompt, :] = _ssd_chunk(xc_ref, z_ref, dtr_ref, y_ref, hst_ref, r0, q_prompt,
                                                    *params).astype(BF16)
        xp_ref[0:SUBLANES, :] = xp_ref[tt:tt + SUBLANES, :]

        @pl.when(i % tps == tps - 1)
        def _():
            convp_ref[0] = xp_ref[tail, :]
            hp_ref[0] = hst_ref[...]

    @pl.when(i >= n_ptiles)
    def _():
        for s in range(tt // seg):
            r0 = s * seg
            xp_ref[tail, :] = conv0_ref[s]
            xp_ref[SUBLANES:SUBLANES + seg, :] = xbc_ref[r0:r0 + seg, :]
            xc_ref[r0:r0 + seg, :] = _silu(_causal_conv(xp_ref, seg, cw_ref, cb_ref))
            hst_ref[...] = h0_ref[s]
            o_ref[r0:r0 + seg, :] = _ssd_chunk(xc_ref, z_ref, dtr_ref, y_ref, hst_ref, r0, seg,
                                               *params).astype(BF16)
            convs_ref[s] = xp_ref[seg + SUBLANES - (CONV_W - 1):seg + SUBLANES, :]
            hs_ref[s] = hst_ref[...]


def _ssd(rest, conv0, h0, cw, cb, dtb, alog, dsk, gn, expand, tril, eye, n_pseq, seq_len, n_sseq, seg):
    t = rest.shape[0]
    tt = TOKEN_TILE
    tps = seq_len // tt
    n_ptiles = n_pseq * tps
    spt = tt // seg
    pseq, stile = _mixer_specs(n_ptiles, tps, spt)
    fixed = lambda i: (0, 0)
    hshape = (N_HEADS, HEAD_DIM, SSM_STATE)
    return pl.pallas_call(
        functools.partial(_ssd_body, n_ptiles, tps, seg, SSD_CHUNK),
        grid=(t // tt,),
        in_specs=[pl.BlockSpec((tt, SSM_CONV_DIM), lambda i: (i, REST_XBC // SSM_CONV_DIM)),
                  pl.BlockSpec((tt, WIDTH), lambda i: (i, REST_ZC // WIDTH)),
                  pl.BlockSpec((tt, LANES), lambda i: (i, REST_DT // LANES)),
                  pl.BlockSpec((spt, CONV_W - 1, SSM_CONV_DIM), lambda i: (stile(i), 0, 0)),
                  pl.BlockSpec((spt,) + hshape, lambda i: (stile(i), 0, 0, 0)),
                  pl.BlockSpec((CONV_W, SSM_CONV_DIM), fixed), pl.BlockSpec((1, SSM_CONV_DIM), fixed),
                  pl.BlockSpec((1, LANES), fixed), pl.BlockSpec((1, LANES), fixed),
                  pl.BlockSpec((1, WIDTH), fixed), pl.BlockSpec((1, WIDTH), fixed),
                  pl.BlockSpec((LANES, WIDTH), fixed), pl.BlockSpec((KEY_BLOCK, KEY_BLOCK), fixed),
                  pl.BlockSpec((SUBLANES, LANES), fixed)],
        out_specs=[pl.BlockSpec((tt, WIDTH), lambda i: (i, 0)),
                   pl.BlockSpec((1, CONV_W - 1, SSM_CONV_DIM), lambda i: (pseq(i), 0, 0)),
                   pl.BlockSpec((1,) + hshape, lambda i: (pseq(i), 0, 0, 0)),
                   pl.BlockSpec((spt, CONV_W - 1, SSM_CONV_DIM), lambda i: (stile(i), 0, 0)),
                   pl.BlockSpec((spt,) + hshape, lambda i: (stile(i), 0, 0, 0))],
        out_shape=[jax.ShapeDtypeStruct((t, WIDTH), BF16),
                   jax.ShapeDtypeStruct((n_pseq, CONV_W - 1, SSM_CONV_DIM), F32),
                   jax.ShapeDtypeStruct((n_pseq,) + hshape, F32),
                   jax.ShapeDtypeStruct((n_sseq, CONV_W - 1, SSM_CONV_DIM), F32),
                   jax.ShapeDtypeStruct((n_sseq,) + hshape, F32)],
        scratch_shapes=[pltpu.VMEM((tt + SUBLANES, SSM_CONV_DIM), F32), pltpu.VMEM((tt, SSM_CONV_DIM), F32),
                        pltpu.VMEM((tt, WIDTH), F32), pltpu.VMEM(hshape, F32)],
        compiler_params=_cparams(("arbitrary",)),
        name="ssd",
    )(rest, rest, rest, conv0, h0, cw, cb, dtb, alog, dsk, gn, expand, tril, eye)


def _route(lt, be_ref, bg_ref):
    n = lt.shape[1]
    le = lt[0:N_EXPERTS, :] + be_ref[:, 0:1]
    lg = lt[N_EXPERTS:N_EXPERTS + SUBLANES, :] + bg_ref[:, 0:1]
    gmax = jnp.max(lg, axis=0, keepdims=True)
    gi = _iota((SUBLANES, n), 0)
    g_sel = jnp.min(jnp.where(lg == gmax, gi, SUBLANES), axis=0, keepdims=True)
    p_sel = 1.0 / jnp.sum(jnp.exp(lg - gmax), axis=0, keepdims=True)
    ei = _iota((N_EXPERTS, n), 0)
    m1 = jnp.where((ei >> 3) == g_sel, le, -jnp.inf)
    v1 = jnp.max(m1, axis=0, keepdims=True)
    i1 = jnp.min(jnp.where(m1 == v1, ei, N_EXPERTS), axis=0, keepdims=True)
    m2 = jnp.where(ei == i1, -jnp.inf, m1)
    v2 = jnp.max(m2, axis=0, keepdims=True)
    i2 = jnp.min(jnp.where(m2 == v2, ei, N_EXPERTS), axis=0, keepdims=True)
    e2 = jnp.exp(v2 - v1)
    w1 = p_sel / (1.0 + e2)
    w2 = w1 * e2
    r = _iota((SUBLANES, n), 0)
    out = jnp.where(r == 0, i1.astype(F32), 0.0)
    out = jnp.where(r == 1, i2.astype(F32), out)
    out = jnp.where(r == 2, w1, out)
    return jnp.where(r == 3, w2, out)


def _merge_body(n_ptiles, h_ref, oap_ref, oas_ref, ob_ref, oc_ref, gmix_ref, wg_ref, bg_ref, wbo_ref, wout_ref,
                gffn_ref, wr_ref, bre_ref, brg_ref, h1_ref, u2_ref, route_ref, ya_ref):
    i = pl.program_id(0)
    x = h_ref[...]
    ub = _rmsnorm(x, gmix_ref[...]).astype(BF16)

    @pl.when(i < n_ptiles)
    def _():
        ya_ref[...] = _dot(oap_ref[...], wbo_ref[0:WIDTH, :])

    @pl.when(i >= n_ptiles)
    def _():
        ya_ref[...] = _dot(oas_ref[...].astype(BF16), wbo_ref[0:WIDTH, :])

    def gate(b):
        return jax.nn.sigmoid(_dot(ub, wg_ref[:, D_MODEL * b:D_MODEL * (b + 1)])
                              + bg_ref[:, D_MODEL * b:D_MODEL * (b + 1)])

    merged = gate(0) * ya_ref[...]
    merged = merged + gate(1) * _dot(ob_ref[...], wbo_ref[WIDTH:2 * WIDTH, :])
    merged = merged + gate(2) * _dot(oc_ref[...], wbo_ref[2 * WIDTH:3 * WIDTH, :])
    h1 = x + _dot(merged.astype(BF16), wout_ref[...])
    h1_ref[...] = h1
    u2 = _rmsnorm(h1, gffn_ref[...])
    u2_ref[...] = u2
    lt = _dot_nt(wr_ref[...].astype(BF16), u2.astype(BF16))
    route_ref[...] = _route(lt, bre_ref, brg_ref)


def _merge(h, oa_p, oa_s, ob, oc, gmix, wg, bg, wbo, wout, gffn, wr, bre, brg):
    t = h.shape[0]
    tm = PROJ_TILE
    n_ptiles = oa_p.shape[0] // tm
    row = lambda i: (i, 0)
    fixed = lambda i: (0, 0)
    full = lambda a: pl.BlockSpec(a.shape, fixed)
    return pl.pallas_call(
        functools.partial(_merge_body, n_ptiles),
        grid=(t // tm,),
        in_specs=[pl.BlockSpec((tm, D_MODEL), row),
                  pl.BlockSpec((tm, WIDTH), lambda i: (jnp.minimum(i, n_ptiles - 1), 0)),
                  pl.BlockSpec((tm, WIDTH), lambda i: (jnp.maximum(i - n_ptiles, 0), 0)),
                  pl.BlockSpec((tm, WIDTH), row), pl.BlockSpec((tm, WIDTH), row),
                  full(gmix), full(wg), full(bg), full(wbo), full(wout), full(gffn), full(wr), full(bre), full(brg)],
        out_specs=[pl.BlockSpec((tm, D_MODEL), row), pl.BlockSpec((tm, D_MODEL), row),
                   pl.BlockSpec((SUBLANES, tm), lambda i: (0, i))],
        out_shape=[jax.ShapeDtypeStruct((t, D_MODEL), F32), jax.ShapeDtypeStruct((t, D_MODEL), F32),
                   jax.ShapeDtypeStruct((SUBLANES, t), F32)],
        scratch_shapes=[pltpu.VMEM((tm, D_MODEL), F32)],
        compiler_params=_cparams(("arbitrary",)),
        name="merge",
    )(h, oa_p, oa_s, ob, oc, gmix, wg, bg, wbo, wout, gffn, wr, bre, brg)


def _rank_body(route_ref, before_ref, ltri_ref, dest_ref, meta_ref, cnt_ref, start_ref):
    p = pl.program_id(0)
    i = pl.program_id(1)
    tm = route_ref.shape[1]
    ei = _iota((N_EXPERTS, tm), 0)
    hot0 = ei == route_ref[0:1, :].astype(I32)
    hot1 = ei == route_ref[1:2, :].astype(I32)
    both = jnp.where(jnp.logical_or(hot0, hot1), 1.0, 0.0)
    tile_cnt = jnp.broadcast_to(jnp.sum(both, axis=1, keepdims=True), (N_EXPERTS, LANES))

    @pl.when(jnp.logical_and(p == 0, i == 0))
    def _():
        cnt_ref[...] = jnp.zeros_like(cnt_ref)

    @pl.when(jnp.logical_and(p == 1, i == 0))
    def _():
        cnt = cnt_ref[...]
        padded = jnp.floor((cnt + (EXPERT_BLOCK - 1)) * (1.0 / EXPERT_BLOCK)) * EXPERT_BLOCK
        start = _sel_left(ltri_ref[...], padded)
        start_ref[...] = start
        lane = _iota((N_EXPERTS, LANES), 1)
        meta_ref[...] = jnp.where(lane == 0, cnt, jnp.where(lane == 1, start, 0.0))
        cnt_ref[...] = jnp.zeros_like(cnt_ref)

    @pl.when(p == 1)
    def _():
        prior = _dot(both.astype(BF16), before_ref[...])
        slot = prior + (start_ref[:, 0:1] + cnt_ref[:, 0:1])
        d0 = jnp.sum(jnp.where(hot0, slot, 0.0), axis=0, keepdims=True)
        d1 = jnp.sum(jnp.where(hot1, slot, 0.0), axis=0, keepdims=True)
        dest_ref[0] = jnp.concatenate([d0, d1], axis=0).astype(I32)

    cnt_ref[...] = cnt_ref[...] + tile_cnt


def _rank(route, before, ltri):
    t = route.shape[1]
    tm = TOKEN_TILE
    nt = t // tm
    return pl.pallas_call(
        _rank_body,
        grid=(2, nt),
        in_specs=[pl.BlockSpec((SUBLANES, tm), lambda p, i: (0, i)),
                  pl.BlockSpec((tm, tm), lambda p, i: (0, 0)),
                  pl.BlockSpec((N_EXPERTS, N_EXPERTS), lambda p, i: (0, 0))],
        out_specs=[pl.BlockSpec((1, 2, tm), lambda p, i: (i * p, 0, 0)),
                   pl.BlockSpec((N_EXPERTS, LANES), lambda p, i: (0, 0))],
        out_shape=[jax.ShapeDtypeStruct((nt, 2, tm), I32), jax.ShapeDtypeStruct((N_EXPERTS, LANES), F32)],
        scratch_shapes=[pltpu.VMEM((N_EXPERTS, LANES), F32), pltpu.VMEM((N_EXPERTS, LANES), F32)],
        compiler_params=_cparams(("arbitrary", "arbitrary")),
        name="moe_rank",
    )(route, before, ltri)


def _row_copy(src_ref, src_row, dst_ref, dst_row, sem):
    return pltpu.make_async_copy(src_ref.at[pl.ds(src_row, 1), :], dst_ref.at[pl.ds(dst_row, 1), :], sem)


def _dispatch_body(dest_ref, u_ref, xs_in_ref, xs_ref, sem):
    del xs_in_ref
    tm = u_ref.shape[0]

    def start(t, c):
        for k in range(2):
            _row_copy(u_ref, t, xs_ref, dest_ref[0, k, t], sem).start()
        return c

    def wait(t, c):
        for k in range(2):
            _row_copy(u_ref, t, xs_ref, dest_ref[0, k, t], sem).wait()
        return c

    lax.fori_loop(0, tm, start, 0)
    lax.fori_loop(0, tm, wait, 0)


def _dispatch(dest, u2, xs_zero):
    t = u2.shape[0]
    tm = dest.shape[2]
    return pl.pallas_call(
        _dispatch_body,
        grid=(t // tm,),
        in_specs=[pl.BlockSpec((1, 2, tm), lambda i: (i, 0, 0), memory_space=pltpu.SMEM),
                  pl.BlockSpec((tm, D_MODEL), lambda i: (i, 0)),
                  pl.BlockSpec(memory_space=pl.ANY)],
        out_specs=pl.BlockSpec(memory_space=pl.ANY),
        out_shape=jax.ShapeDtypeStruct(xs_zero.shape, F32),
        scratch_shapes=[pltpu.SemaphoreType.DMA(())],
        input_output_aliases={2: 0},
        compiler_params=_cparams(("arbitrary",)),
        name="moe_dispatch",
    )(dest, u2, xs_zero)


def _experts_body(be_ref, nu_ref, x_ref, wg_ref, wu_ref, wd_ref, y_ref):
    del be_ref
    live = pl.program_id(0) < nu_ref[0]

    @pl.when(live)
    def _():
        xb = x_ref[...].astype(BF16)
        a = _silu(_dot(xb, wg_ref[0])) * _dot(xb, wu_ref[0])
        y_ref[...] = _dot(a.astype(BF16), wd_ref[0])

    @pl.when(jnp.logical_not(live))
    def _():
        y_ref[...] = jnp.zeros_like(y_ref)


def _experts(blk_expert, n_used, xs, wg, wu, wd):
    cap = xs.shape[0]
    nb = cap // EXPERT_BLOCK
    live = lambda i, nu: jnp.minimum(i, nu[0] - 1)
    rows = lambda i, be, nu: (live(i, nu), 0)
    wsel = lambda i, be, nu: (be[live(i, nu)], 0, 0)
    return pl.pallas_call(
        _experts_body,
        grid_spec=pltpu.PrefetchScalarGridSpec(
            num_scalar_prefetch=2,
            grid=(nb,),
            in_specs=[pl.BlockSpec((EXPERT_BLOCK, D_MODEL), rows),
                      pl.BlockSpec((1, D_MODEL, D_EXPERT), wsel), pl.BlockSpec((1, D_MODEL, D_EXPERT), wsel),
                      pl.BlockSpec((1, D_EXPERT, D_MODEL), wsel)],
            out_specs=pl.BlockSpec((EXPERT_BLOCK, D_MODEL), lambda i, be, nu: (i, 0))),
        out_shape=jax.ShapeDtypeStruct((cap, D_MODEL), F32),
        compiler_params=_cparams(("arbitrary",)),
        name="moe_experts",
    )(blk_expert, n_used, xs, wg, wu, wd)


def _combine_body(final, per, dest_ref, h_ref, route_ref, g_ref, y_hbm, o_ref, ybuf_ref, sem):
    i = pl.program_id(0)
    tm = h_ref.shape[0]
    base = (i % per) * tm

    def start(t, c):
        for k in range(2):
            _row_copy(y_hbm, dest_ref[0, k, base + t], ybuf_ref.at[k], t, sem).start()
        return c

    def wait(t, c):
        for k in range(2):
            _row_copy(y_hbm, dest_ref[0, k, base + t], ybuf_ref.at[k], t, sem).wait()
        return c

    lax.fori_loop(0, tm, start, 0)
    eye = _iota((tm, tm), 0) == _iota((tm, tm), 1)
    w0 = jnp.sum(jnp.where(eye, route_ref[2:3, :], 0.0), axis=1, keepdims=True)
    w1 = jnp.sum(jnp.where(eye, route_ref[3:4, :], 0.0), axis=1, keepdims=True)
    lax.fori_loop(0, tm, wait, 0)
    h2 = h_ref[...] + (w0 * ybuf_ref[0] + w1 * ybuf_ref[1])
    o_ref[...] = _rmsnorm(h2, g_ref[...]) if final else h2


def _combine(dest, h1, route, g_final, y, final):
    t = h1.shape[0]
    tm = PROJ_TILE
    per = dest.shape[2] // tm
    return pl.pallas_call(
        functools.partial(_combine_body, final, per),
        grid=(t // tm,),
        in_specs=[pl.BlockSpec((1, 2, dest.shape[2]), lambda i: (i // per, 0, 0), memory_space=pltpu.SMEM),
                  pl.BlockSpec((tm, D_MODEL), lambda i: (i, 0)),
                  pl.BlockSpec((SUBLANES, tm), lambda i: (0, i)),
                  pl.BlockSpec((1, D_MODEL), lambda i: (0, 0)),
                  pl.BlockSpec(memory_space=pl.ANY)],
        out_specs=pl.BlockSpec((tm, D_MODEL), lambda i: (i, 0)),
        out_shape=jax.ShapeDtypeStruct((t, D_MODEL), F32),
        scratch_shapes=[pltpu.VMEM((2, tm, D_MODEL), F32), pltpu.SemaphoreType.DMA(())],
        compiler_params=_cparams(("arbitrary",)),
        name="moe_combine",
    )(dest, h1, route, g_final, y)


def _constants():
    r = jnp.arange(3 * KEY_BLOCK)
    c = jnp.arange(2 * KEY_BLOCK)
    tri = jnp.where(c[None, :] < KEY_BLOCK, (r[:, None] % KEY_BLOCK) > c[None, :], True).astype(BF16)
    q = jnp.arange(KEY_BLOCK)
    tril = (q[:, None] >= q[None, :]).astype(BF16)
    lane = jnp.arange(WIDTH)
    expand = (jnp.arange(LANES)[:, None] == lane[None, :] // HEAD_DIM).astype(BF16)
    eye = (jnp.arange(SUBLANES)[:, None] == jnp.arange(LANES)[None, :]).astype(BF16)
    t = jnp.arange(TOKEN_TILE)
    before = (t[:, None] < t[None, :]).astype(BF16)
    e = jnp.arange(N_EXPERTS)
    ltri = (e[:, None] > e[None, :]).astype(BF16)
    return tri, tril, expand, eye, before, ltri


def _block_diag(w):
    n, d = w.shape[0], w.shape[1]
    out = jnp.zeros((n, d, n, d), w.dtype)
    out = out.at[jnp.arange(n), :, jnp.arange(n), :].set(w)
    return out.reshape(n * d, n * d)


def _proj_weight(w_in):
    wq = w_in[:, 0:WIDTH].reshape(D_MODEL, N_HEADS, 1, HEAD_DIM)
    half = (jnp.arange(N_HEADS) % 2)[None, :, None, None] == jnp.arange(2)[None, None, :, None]
    wqm = jnp.where(half, wq, 0.0).reshape(D_MODEL, QM_W)
    o = 3 * WIDTH
    xb, gb, zc = w_in[:, o:o + WIDTH], w_in[:, o + WIDTH:o + 2 * WIDTH], w_in[:, o + 2 * WIDTH:o + 3 * WIDTH]
    xbc = w_in[:, o + 3 * WIDTH:o + 3 * WIDTH + SSM_CONV_DIM]
    dt = jnp.pad(w_in[:, o + 3 * WIDTH + SSM_CONV_DIM:], ((0, 0), (0, LANES - N_HEADS)))
    return jnp.concatenate([wqm, w_in[:, WIDTH:3 * WIDTH], xbc, xb, gb, zc, dt], axis=1).astype(BF16)


def _pad_lanes(v):
    return jnp.pad(v, (0, LANES - v.shape[0]))[None, :]


def kernel(x_prompt, x_sample, cache_sb_k, cache_sb_v, state_lru_conv, state_lru_h, state_ssm_conv, state_ssm_h, g_mix, w_in, w_gate, b_gate, w_branch_out, w_out, lru_conv_w, lru_conv_b, lru_w_a, lru_b_a, lru_w_x, lru_b_x, lru_lambda, ssm_conv_w, ssm_conv_b, ssm_dt_bias, ssm_a_log, ssm_d, ssm_norm_g, g_ffn, w_router_group, b_router_group, w_router_expert, b_router_expert, w_expert_gate, w_expert_up, w_expert_down, g_final):
    n_p, s_p, _ = x_prompt.shape
    n_s, s_s, _ = x_sample.shape
    depth = w_in.shape[0]
    past = cache_sb_k.shape[2]
    t_p, t_s = n_p * s_p, n_s * s_s
    t = t_p + t_s
    assert s_p % TOKEN_TILE == 0 and TOKEN_TILE % s_s == 0 and t_s % TOKEN_TILE == 0
    assert s_s % SUBLANES == 0 and past % SAMPLE_WINDOW == 0

    tri, tril, expand, eye, before, ltri = _constants()
    cache_k = cache_sb_k.reshape(depth, n_s, past, WIDTH)
    cache_v = cache_sb_v.reshape(depth, n_s, past, WIDTH)
    n_blocks = -(-2 * t // EXPERT_BLOCK) + N_EXPERTS
    xs_zero = jnp.zeros((n_blocks * EXPERT_BLOCK, D_MODEL), F32)

    h = jnp.concatenate([x_prompt.reshape(t_p, D_MODEL), x_sample.reshape(t_s, D_MODEL)], axis=0)
    states = []
    for l in range(depth):
        row = lambda a: a[l][None, :]
        qm, k, v, kb, vb, rest = _inproj(h, row(g_mix), _proj_weight(w_in[l]))

        oa_p = _attn_prompt(qm, kb, vb, tri, n_p, s_p)
        acc, rp = _attn_sample(qm, kb, vb, cache_k, cache_v, tri, l, t_p, n_s, s_s)
        if past > SAMPLE_WINDOW:
            older = functools.partial(_attn_older, qm, cache_k, cache_v, tri=tri, layer=l, row0=t_p, n_seq=n_s,
                                      tq=s_s)
            oa_s = lax.cond(jnp.max(rp[:, :N_HEADS]) > LOG_CUTOFF,
                            lambda a, r: older(acc=a, rp=r), lambda a, r: a, acc, rp)
        else:
            oa_s = acc

        ob, lconv_p, lh_p, lconv_s, lh_s = _lru(
            rest, state_lru_conv[l], state_lru_h[l], lru_conv_w[l], row(lru_conv_b),
            _block_diag(lru_w_a[l]).astype(BF16), row(lru_b_a), _block_diag(lru_w_x[l]).astype(BF16),
            row(lru_b_x), row(lru_lambda), n_p, s_p, n_s, s_s)

        oc, sconv_p, sh_p, sconv_s, sh_s = _ssd(
            rest, state_ssm_conv[l], state_ssm_h[l], ssm_conv_w[l], row(ssm_conv_b),
            _pad_lanes(ssm_dt_bias[l]), _pad_lanes(ssm_a_log[l]), jnp.repeat(ssm_d[l], HEAD_DIM)[None, :],
            row(ssm_norm_g), expand, tril, eye, n_p, s_p, n_s, s_s)

        w_r = jnp.concatenate([w_router_expert[l].T, w_router_group[l].T,
                               jnp.zeros((SUBLANES - N_GROUPS, D_MODEL), F32)], axis=0)
        b_re = jnp.broadcast_to(b_router_expert[l][:, None], (N_EXPERTS, LANES))
        b_rg = jnp.broadcast_to(jnp.concatenate([b_router_group[l], jnp.full((SUBLANES - N_GROUPS,), -1e30, F32)])[:, None],
                                (SUBLANES, LANES))
        h1, u2, route = _merge(h, oa_p, oa_s, ob, oc, row(g_mix), w_gate[l].astype(BF16), row(b_gate),
                               w_branch_out[l].astype(BF16), w_out[l].astype(BF16), row(g_ffn), w_r, b_re, b_rg)

        dest, meta = _rank(route, before, ltri)
        counts, starts = meta[:, 0], meta[:, 1]
        ends = starts + jnp.ceil(counts / EXPERT_BLOCK) * EXPERT_BLOCK
        blk_row = (jnp.arange(n_blocks) * EXPERT_BLOCK).astype(F32)
        blk_expert = jnp.minimum(jnp.sum(ends[None, :] <= blk_row[:, None], axis=1), N_EXPERTS - 1).astype(I32)
        n_used = (ends[N_EXPERTS - 1:] / EXPERT_BLOCK).astype(I32)

        xs = _dispatch(dest, u2, xs_zero)
        y = _experts(blk_expert, n_used, xs, w_expert_gate[l].astype(BF16), w_expert_up[l].astype(BF16),
                     w_expert_down[l].astype(BF16))
        h = _combine(dest, h1, route, g_final[None, :], y, final=(l == depth - 1))

        head = lambda a, n, s: a.reshape(n, s, N_HEADS, HEAD_DIM)
        states.append((head(k[:t_p], n_p, s_p), head(v[:t_p], n_p, s_p), lconv_p, lh_p[:, 0], sconv_p, sh_p,
                       head(k[t_p:], n_s, s_s), head(v[t_p:], n_s, s_s), lconv_s, lh_s, sconv_s, sh_s))

    stacked = tuple(jnp.stack([st[j] for st in states], axis=0) for j in range(12))
    return (h[:t_p].reshape(n_p, s_p, D_MODEL), h[t_p:].reshape(n_s, s_s, D_MODEL)) + stacked
```

```python
import functools

import jax
import jax.numpy as jnp
from jax import lax
from jax.experimental import pallas as pl
from jax.experimental.pallas import tpu as pltpu

F32 = jnp.float32
BF16 = jnp.bfloat16
I32 = jnp.int32

D_MODEL = 1024
N_HEADS = 8
HEAD_DIM = 64
WIDTH = 512
CONV_W = 4
SSM_STATE = 128
SSM_CONV_DIM = 1024
N_GROUPS = 4
PER_GROUP = 8
N_EXPERTS = 32
D_EXPERT = 512
LRU_C = 8.0
EPS = 1e-6

LANES = 128
SUBLANES = 8
TOKEN_TILE = 512
PROJ_TILE = 256
KEY_BLOCK = 128
SSD_CHUNK = 64
SAMPLE_WINDOW = 512
EXPERT_BLOCK = 256
VMEM_LIMIT = 56 * 1024 * 1024

LOG_CUTOFF = -88.0

REST_XBC, REST_XB, REST_GB, REST_ZC, REST_DT = 0, 1024, 1536, 2048, 2560
REST_W = 2688
QM_W = 1024
PROJ_W = QM_W + 2 * WIDTH + REST_W


def _cparams(sem, **kw):
    return pltpu.CompilerParams(dimension_semantics=sem, vmem_limit_bytes=VMEM_LIMIT, **kw)


ROW_DMA_UNROLL = 8


def _split3(x):
    hi = x.astype(BF16)
    r = x - hi.astype(F32)
    mid = r.astype(BF16)
    lo = (r - mid.astype(F32)).astype(BF16)
    return hi, mid, lo


def _dot(a, b):
    return jnp.dot(a, b, preferred_element_type=F32)


def _dot_nt(a, b):
    return lax.dot_general(a, b, (((1,), (1,)), ((), ())), preferred_element_type=F32)


def _dot_tn(a, b):
    return lax.dot_general(a, b, (((0,), (0,)), ((), ())), preferred_element_type=F32)


def _sel_right(x, m01):
    return sum(_dot(p, m01) for p in _split3(x))


def _sel_left(m01, x):
    return sum(_dot(m01, p) for p in _split3(x))


def _rmsnorm(x, g):
    return x * lax.rsqrt(jnp.mean(x * x, axis=-1, keepdims=True) + EPS) * g


def _log_sigmoid(z):
    return jnp.minimum(z, 0.0) - jnp.log1p(jnp.exp(-jnp.abs(z)))


def _silu(x):
    return x * jax.nn.sigmoid(x)


def _iota(shape, dim):
    return lax.broadcasted_iota(I32, shape, dim)


def _store_heads(ref, x):
    rows = x.shape[0]
    for h in range(N_HEADS):
        ref[pl.ds(h, rows, stride=N_HEADS), :] = x[:, HEAD_DIM * h:HEAD_DIM * (h + 1)]


def _inproj_body(n_ptiles, h_ref, g_ref, w_ref, qm_ref, kp_ref, vp_ref, ks_ref, vs_ref, kb_ref, vb_ref, rest_ref):
    i = pl.program_id(0)
    ub = _rmsnorm(h_ref[...], g_ref[...]).astype(BF16)

    def proj(c0, c1):
        return _dot(ub, w_ref[:, c0:c1])

    for c in range(0, QM_W, 512):
        qm_ref[:, c:c + 512] = proj(c, c + 512).astype(BF16)
    k = proj(QM_W, QM_W + WIDTH)
    kb_ref[...] = k.astype(BF16)
    v = proj(QM_W + WIDTH, QM_W + 2 * WIDTH)
    vb_ref[...] = v.astype(BF16)

    @pl.when(i < n_ptiles)
    def _():
        _store_heads(kp_ref, k)
        _store_heads(vp_ref, v)

    @pl.when(i >= n_ptiles)
    def _():
        _store_heads(ks_ref, k)
        _store_heads(vs_ref, v)

    base = QM_W + 2 * WIDTH
    for c in range(0, REST_W, 512):
        c1 = min(c + 512, REST_W)
        rest_ref[:, c:c1] = proj(base + c, base + c1)


def _inproj(h, g, w, t_p):
    t = h.shape[0]
    tm = PROJ_TILE
    n_ptiles = t_p // tm
    row = lambda i: (i, 0)
    fixed = lambda i: (0, 0)
    prow = pl.BlockSpec((tm * N_HEADS, HEAD_DIM), lambda i: (jnp.minimum(i, n_ptiles - 1), 0))
    srow = pl.BlockSpec((tm * N_HEADS, HEAD_DIM), lambda i: (jnp.maximum(i - n_ptiles, 0), 0))
    heads = lambda n: jax.ShapeDtypeStruct((n * N_HEADS, HEAD_DIM), F32)
    return pl.pallas_call(
        functools.partial(_inproj_body, n_ptiles),
        grid=(t // tm,),
        in_specs=[pl.BlockSpec((tm, D_MODEL), row), pl.BlockSpec((1, D_MODEL), fixed),
                  pl.BlockSpec((D_MODEL, PROJ_W), fixed)],
        out_specs=[pl.BlockSpec((tm, QM_W), row), prow, prow, srow, srow, pl.BlockSpec((tm, WIDTH), row),
                   pl.BlockSpec((tm, WIDTH), row), pl.BlockSpec((tm, REST_W), row)],
        out_shape=[jax.ShapeDtypeStruct((t, QM_W), BF16), heads(t_p), heads(t_p), heads(t - t_p), heads(t - t_p),
                   jax.ShapeDtypeStruct((t, WIDTH), BF16), jax.ShapeDtypeStruct((t, WIDTH), BF16),
                   jax.ShapeDtypeStruct((t, REST_W), F32)],
        compiler_params=_cparams(("arbitrary",)),
        name="inproj",
    )(h, g, w)


def _sb_weights(zs, mask, r_ref, tri_ref):
    kb = zs[0].shape[1]
    log_beta, log_keep = [], []
    for z in zs:
        ls = _log_sigmoid(z)
        lk = ls - z
        log_beta.append(ls)
        log_keep.append(lk if mask is None else jnp.where(mask, lk, 0.0))
    sums = []
    for lk in log_keep:
        if kb == KEY_BLOCK:
            res = _dot(jnp.concatenate(_split3(lk), axis=1), tri_ref[...])
            sums.append((res[:, :KEY_BLOCK], res[:, KEY_BLOCK:]))
        else:
            sums.append((_sel_right(lk, tri_ref[0:kb, 0:kb]),
                         _sel_right(lk, tri_ref[0:kb, KEY_BLOCK:2 * KEY_BLOCK])))
    ws, top = [], None
    for h, (ls, (la, tot)) in enumerate(zip(log_beta, sums)):
        r_old = r_ref[h]
        w = jnp.exp(ls + la + r_old[:, :kb])
        ws.append((w if mask is None else jnp.where(mask, w, 0.0)).astype(BF16))
        r_new = r_old + tot
        r_ref[h] = r_new
        top = r_new if top is None else jnp.maximum(top, r_new)
    return ws, jnp.max(top)


def _sb_block_pairs(qm, kblk, vblk, mask, acc_ref, r_ref, tri_ref):
    pair = lambda x, h: x[:, LANES * (h // 2):LANES * (h // 2 + 1)]
    zs = [_dot_nt(qm[:, LANES * h:LANES * (h + 1)], pair(kblk, h)) * (HEAD_DIM ** -0.5) for h in range(N_HEADS)]
    ws, rmax = _sb_weights(zs, mask, r_ref, tri_ref)
    low = _iota((qm.shape[0], LANES), 1) < HEAD_DIM
    for p in range(N_HEADS // 2):
        v2 = pair(vblk, 2 * p)
        acc_ref[:, LANES * p:LANES * (p + 1)] += jnp.where(low, _dot(ws[2 * p], v2), _dot(ws[2 * p + 1], v2))
    return rmax


def _sb_block_heads(q_heads, k_of, v_of, mask, acc_ref, r_ref, tri_ref):
    zs = [_dot_nt(q_heads[h], k_of(h)) * (HEAD_DIM ** -0.5) for h in range(N_HEADS)]
    ws, rmax = _sb_weights(zs, mask, r_ref, tri_ref)
    for h in range(N_HEADS):
        acc_ref[:, HEAD_DIM * h:HEAD_DIM * (h + 1)] += _dot(ws[h], v_of(h))
    return rmax


def _query_heads(qm):
    return [qm[:, LANES * h + HEAD_DIM * (h % 2):LANES * h + HEAD_DIM * (h % 2 + 1)] for h in range(N_HEADS)]


def _cache_heads(c_ref, block):
    base = pl.multiple_of(block * (KEY_BLOCK * N_HEADS), KEY_BLOCK * N_HEADS)
    return lambda h: c_ref[0, 0, pl.ds(base + h, KEY_BLOCK, stride=N_HEADS), :].astype(BF16)


def _more_keys(c):
    j, rmax = c
    return jnp.logical_and(j >= 0, rmax > LOG_CUTOFF)


def _attn_prompt_body(qm_ref, kb_ref, vb_ref, tri_ref, o_ref, acc_ref, r_ref):
    i = pl.program_id(1)
    tq = qm_ref.shape[0]
    acc_ref[...] = jnp.zeros_like(acc_ref)
    r_ref[...] = jnp.zeros_like(r_ref)
    qm = qm_ref[...]

    def block(j, mask):
        off = pl.multiple_of(j * KEY_BLOCK, KEY_BLOCK)
        return _sb_block_pairs(qm, kb_ref[pl.ds(off, KEY_BLOCK), :], vb_ref[pl.ds(off, KEY_BLOCK), :],
                               mask, acc_ref, r_ref, tri_ref)

    rmax = block(i, _iota((tq, tq), 1) < _iota((tq, tq), 0))
    lax.while_loop(_more_keys, lambda c: (c[0] - 1, block(c[0], None)), (i - 1, rmax))
    o_ref[...] = acc_ref[...].astype(BF16)


def _attn_prompt(qm, kb, vb, tri, n_seq, seq_len):
    tq = KEY_BLOCK
    nq = seq_len // tq
    return pl.pallas_call(
        _attn_prompt_body,
        grid=(n_seq, nq),
        in_specs=[pl.BlockSpec((tq, QM_W), lambda b, i: (b * nq + i, 0)),
                  pl.BlockSpec((seq_len, WIDTH), lambda b, i: (b, 0)),
                  pl.BlockSpec((seq_len, WIDTH), lambda b, i: (b, 0)),
                  pl.BlockSpec((3 * KEY_BLOCK, 2 * KEY_BLOCK), lambda b, i: (0, 0))],
        out_specs=pl.BlockSpec((tq, WIDTH), lambda b, i: (b * nq + i, 0)),
        out_shape=jax.ShapeDtypeStruct((n_seq * seq_len, WIDTH), BF16),
        scratch_shapes=[pltpu.VMEM((tq, WIDTH), F32), pltpu.VMEM((N_HEADS, tq, LANES), F32)],
        compiler_params=_cparams(("arbitrary", "arbitrary")),
        name="attn_prompt",
    )(qm, kb, vb, tri)


def _pack_r(r_ref, tq):
    lane = _iota((tq, LANES), 1)
    rp = jnp.zeros((tq, LANES), F32)
    for h in range(N_HEADS):
        rp = jnp.where(lane == h, r_ref[h], rp)
    return rp


def _attn_sample_body(qm_ref, kn_ref, vn_ref, ck_ref, cv_ref, tri_ref, acc_out, rp_out, acc_ref, r_ref):
    tq = qm_ref.shape[0]
    acc_ref[...] = jnp.zeros_like(acc_ref)
    r_ref[...] = jnp.zeros_like(r_ref)
    q_heads = _query_heads(qm_ref[...])
    kn, vn = kn_ref[...], vn_ref[...]
    head = lambda x: (lambda h: x[:, HEAD_DIM * h:HEAD_DIM * (h + 1)])
    causal = _iota((tq, tq), 1) < _iota((tq, tq), 0)
    rmax = _sb_block_heads(q_heads, head(kn), head(vn), causal, acc_ref, r_ref, tri_ref)

    def body(c):
        j, _ = c
        return j - 1, _sb_block_heads(q_heads, _cache_heads(ck_ref, j), _cache_heads(cv_ref, j), None,
                                      acc_ref, r_ref, tri_ref)

    n_blocks = ck_ref.shape[2] // (KEY_BLOCK * N_HEADS)
    lax.while_loop(_more_keys, body, (jnp.int32(n_blocks - 1), rmax))
    acc_out[...] = acc_ref[...]
    rp_out[...] = _pack_r(r_ref, tq)


def _attn_sample(qm, kb, vb, cache_k, cache_v, tri, layer, row0, n_seq, tq):
    past = cache_k.shape[2] // N_HEADS
    win = min(SAMPLE_WINDOW, past)
    blk0 = row0 // tq
    cur = lambda b: (blk0 + b, 0)
    cache = lambda b: (layer, b, past // win - 1, 0)
    cblock = pl.BlockSpec((1, 1, win * N_HEADS, HEAD_DIM), cache)
    return pl.pallas_call(
        _attn_sample_body,
        grid=(n_seq,),
        in_specs=[pl.BlockSpec((tq, QM_W), cur), pl.BlockSpec((tq, WIDTH), cur), pl.BlockSpec((tq, WIDTH), cur),
                  cblock, cblock,
                  pl.BlockSpec((3 * KEY_BLOCK, 2 * KEY_BLOCK), lambda b: (0, 0))],
        out_specs=[pl.BlockSpec((tq, WIDTH), lambda b: (b, 0)), pl.BlockSpec((tq, LANES), lambda b: (b, 0))],
        out_shape=[jax.ShapeDtypeStruct((n_seq * tq, WIDTH), F32), jax.ShapeDtypeStruct((n_seq * tq, LANES), F32)],
        scratch_shapes=[pltpu.VMEM((tq, WIDTH), F32), pltpu.VMEM((N_HEADS, tq, LANES), F32)],
        compiler_params=_cparams(("arbitrary",)),
        name="attn_sample",
    )(qm, kb, vb, cache_k, cache_v, tri)


def _attn_older_body(qm_ref, ck_ref, cv_ref, acc_in, rp_in, tri_ref, acc_out, acc_ref, r_ref):
    s = pl.program_id(1)
    tq = qm_ref.shape[0]

    @pl.when(s == 0)
    def _():
        acc_ref[...] = acc_in[...]
        rp = rp_in[...]
        for h in range(N_HEADS):
            r_ref[h] = jnp.broadcast_to(rp[:, h:h + 1], (tq, LANES))

    rmax = jnp.max(r_ref[0])
    for h in range(1, N_HEADS):
        rmax = jnp.maximum(rmax, jnp.max(r_ref[h]))

    @pl.when(rmax > LOG_CUTOFF)
    def _():
        _sb_block_heads(_query_heads(qm_ref[...]), _cache_heads(ck_ref, 0), _cache_heads(cv_ref, 0), None,
                        acc_ref, r_ref, tri_ref)

    @pl.when(s == pl.num_programs(1) - 1)
    def _():
        acc_out[...] = acc_ref[...]


def _attn_older(qm, cache_k, cache_v, acc, rp, tri, layer, row0, n_seq, tq):
    past = cache_k.shape[2] // N_HEADS
    win = min(SAMPLE_WINDOW, past)
    nb = (past - win) // KEY_BLOCK
    blk0 = row0 // tq
    cache = lambda b, s: (layer, b, nb - 1 - s, 0)
    cblock = pl.BlockSpec((1, 1, KEY_BLOCK * N_HEADS, HEAD_DIM), cache)
    return pl.pallas_call(
        _attn_older_body,
        grid=(n_seq, nb),
        in_specs=[pl.BlockSpec((tq, QM_W), lambda b, s: (blk0 + b, 0)), cblock, cblock,
                  pl.BlockSpec((tq, WIDTH), lambda b, s: (b, 0)), pl.BlockSpec((tq, LANES), lambda b, s: (b, 0)),
                  pl.BlockSpec((3 * KEY_BLOCK, 2 * KEY_BLOCK), lambda b, s: (0, 0))],
        out_specs=pl.BlockSpec((tq, WIDTH), lambda b, s: (b, 0)),
        out_shape=jax.ShapeDtypeStruct((n_seq * tq, WIDTH), F32),
        scratch_shapes=[pltpu.VMEM((tq, WIDTH), F32), pltpu.VMEM((N_HEADS, tq, LANES), F32)],
        compiler_params=_cparams(("arbitrary", "arbitrary")),
        name="attn_older",
    )(qm, cache_k, cache_v, acc, rp, tri)


def _causal_conv(xp_ref, n, w_ref, b_ref):
    y = b_ref[...]
    for k in range(CONV_W):
        y = y + xp_ref[pl.ds(SUBLANES - (CONV_W - 1) + k, n), :] * w_ref[k:k + 1, :]
    return y


def _lru_rows(xp_ref, a_ref, b_ref, n, h0, gate, cw_ref, cb_ref, wa_ref, ba_ref, wx_ref, bx_ref, lam_ref):
    xc = _causal_conv(xp_ref, n, cw_ref, cb_ref)
    xcb = xc.astype(BF16)
    r = jax.nn.sigmoid(_dot(xcb, wa_ref[...]) + ba_ref[...])
    ig = jax.nn.sigmoid(_dot(xcb, wx_ref[...]) + bx_ref[...])
    log_a = LRU_C * r * _log_sigmoid(lam_ref[...])
    a_ref[0:n, :] = jnp.exp(log_a)
    th = jnp.tanh(log_a)
    b_ref[0:n, :] = jnp.sqrt(-2.0 * th / (1.0 - th)) * (ig * xc)
    row = _iota((SUBLANES, WIDTH), 0)

    def group(g, h):
        off = pl.multiple_of(g * SUBLANES, SUBLANES)
        a = a_ref[pl.ds(off, SUBLANES), :]
        b = b_ref[pl.ds(off, SUBLANES), :]
        for s in (1, 2, 4):
            a_prev = jnp.where(row >= s, pltpu.roll(a, s, 0), 1.0)
            b_prev = jnp.where(row >= s, pltpu.roll(b, s, 0), 0.0)
            b = b + a * b_prev
            a = a * a_prev
        hs = a * h + b
        b_ref[pl.ds(off, SUBLANES), :] = hs
        return hs[SUBLANES - 1:SUBLANES, :]

    h_last = lax.fori_loop(0, n // SUBLANES, group, h0)
    y = b_ref[0:n, :] * jax.nn.gelu(gate, approximate=True)
    return y, h_last


def _lru_body(n_ptiles, tps, seg, xb_ref, gb_ref, conv0_ref, h0_ref, cw_ref, cb_ref, wa_ref, ba_ref, wx_ref,
              bx_ref, lam_ref, o_ref, convp_ref, hp_ref, convs_ref, hs_ref, xp_ref, a_ref, b_ref, hc_ref):
    i = pl.program_id(0)
    tt = xb_ref.shape[0]
    params = (cw_ref, cb_ref, wa_ref, ba_ref, wx_ref, bx_ref, lam_ref)

    @pl.when(i < n_ptiles)
    def _():
        @pl.when(i % tps == 0)
        def _():
            xp_ref[0:SUBLANES, :] = jnp.zeros((SUBLANES, WIDTH), F32)
            hc_ref[...] = jnp.zeros_like(hc_ref)

        xp_ref[SUBLANES:SUBLANES + tt, :] = xb_ref[...]
        y, h_last = _lru_rows(xp_ref, a_ref, b_ref, tt, hc_ref[...], gb_ref[...], *params)
        o_ref[...] = y.astype(BF16)
        hc_ref[...] = h_last
        xp_ref[0:SUBLANES, :] = xp_ref[tt:tt + SUBLANES, :]

        @pl.when(i % tps == tps - 1)
        def _():
            convp_ref[0] = xp_ref[SUBLANES - (CONV_W - 1):SUBLANES, :]
            hp_ref[0] = h_last

    @pl.when(i >= n_ptiles)
    def _():
        for s in range(tt // seg):
            xp_ref[SUBLANES - (CONV_W - 1):SUBLANES, :] = conv0_ref[s]
            xp_ref[SUBLANES:SUBLANES + seg, :] = xb_ref[s * seg:(s + 1) * seg, :]
            y, h_last = _lru_rows(xp_ref, a_ref, b_ref, seg, h0_ref[s:s + 1, :],
                                  gb_ref[s * seg:(s + 1) * seg, :], *params)
            o_ref[s * seg:(s + 1) * seg, :] = y.astype(BF16)
            convs_ref[s] = xp_ref[seg + SUBLANES - (CONV_W - 1):seg + SUBLANES, :]
            hs_ref[s:s + 1, :] = h_last


def _mixer_specs(n_ptiles, tps, spt):
    pseq = lambda i: jnp.minimum(i, n_ptiles - 1) // tps
    stile = lambda i: jnp.maximum(i - n_ptiles, 0)
    return pseq, stile


def _lru(rest, conv0, h0, cw, cb, wa, ba, wx, bx, lam, n_pseq, seq_len, n_sseq, seg):
    t = rest.shape[0]
    tt = TOKEN_TILE
    tps = seq_len // tt
    n_ptiles = n_pseq * tps
    spt = tt // seg
    pseq, stile = _mixer_specs(n_ptiles, tps, spt)
    fixed = lambda i: (0, 0)
    vec = pl.BlockSpec((1, WIDTH), fixed)
    return pl.pallas_call(
        functools.partial(_lru_body, n_ptiles, tps, seg),
        grid=(t // tt,),
        in_specs=[pl.BlockSpec((tt, WIDTH), lambda i: (i, REST_XB // WIDTH)),
                  pl.BlockSpec((tt, WIDTH), lambda i: (i, REST_GB // WIDTH)),
                  pl.BlockSpec((spt, CONV_W - 1, WIDTH), lambda i: (stile(i), 0, 0)),
                  pl.BlockSpec((spt, WIDTH), lambda i: (stile(i), 0)),
                  pl.BlockSpec((CONV_W, WIDTH), fixed), vec,
                  pl.BlockSpec((WIDTH, WIDTH), fixed), vec, pl.BlockSpec((WIDTH, WIDTH), fixed), vec, vec],
        out_specs=[pl.BlockSpec((tt, WIDTH), lambda i: (i, 0)),
                   pl.BlockSpec((1, CONV_W - 1, WIDTH), lambda i: (pseq(i), 0, 0)),
                   pl.BlockSpec((1, 1, WIDTH), lambda i: (pseq(i), 0, 0)),
                   pl.BlockSpec((spt, CONV_W - 1, WIDTH), lambda i: (stile(i), 0, 0)),
                   pl.BlockSpec((spt, WIDTH), lambda i: (stile(i), 0))],
        out_shape=[jax.ShapeDtypeStruct((t, WIDTH), BF16),
                   jax.ShapeDtypeStruct((n_pseq, CONV_W - 1, WIDTH), F32),
                   jax.ShapeDtypeStruct((n_pseq, 1, WIDTH), F32),
                   jax.ShapeDtypeStruct((n_sseq, CONV_W - 1, WIDTH), F32),
                   jax.ShapeDtypeStruct((n_sseq, WIDTH), F32)],
        scratch_shapes=[pltpu.VMEM((tt + SUBLANES, WIDTH), F32), pltpu.VMEM((tt, WIDTH), F32),
                        pltpu.VMEM((tt, WIDTH), F32), pltpu.VMEM((1, WIDTH), F32)],
        compiler_params=_cparams(("arbitrary",)),
        name="rglru",
    )(rest, rest, conv0, h0, cw, cb, wa, ba, wx, bx, lam)


def _ssd_chunk(xc_ref, z_ref, dtr_ref, y_ref, hst_ref, r0, q, dtb_ref, alog_ref, dsk_ref, gn_ref, exp_ref,
               tril_ref, eye_ref):
    xs = xc_ref[r0:r0 + q, 0:WIDTH]
    bm = xc_ref[r0:r0 + q, WIDTH:WIDTH + 2 * SSM_STATE].astype(BF16)
    cm = xc_ref[r0:r0 + q, WIDTH + 2 * SSM_STATE:SSM_CONV_DIM].astype(BF16)
    dt = jax.nn.softplus(dtr_ref[r0:r0 + q, :] + dtb_ref[...])
    da = dt * (-jnp.exp(alog_ref[...]))
    a_cum = _sel_left(tril_ref[0:q, 0:q], da)
    a_exp = _sel_right(a_cum, exp_ref[...])
    dt_exp = _sel_right(dt, exp_ref[...])
    a_cum_t = sum(_dot_nt(eye_ref[...], p) for p in _split3(a_cum))
    a_last = a_cum[q - 1:q, :]
    xdt = xs * dt_exp
    xdtb = xdt.astype(BF16)
    xw = (xdt * jnp.exp(a_exp[q - 1:q, :] - a_exp)).astype(BF16)
    ea = jnp.exp(a_exp)
    causal = _iota((q, q), 0) >= _iota((q, q), 1)
    for g in range(2):
        bg = bm[:, SSM_STATE * g:SSM_STATE * (g + 1)]
        cg = cm[:, SSM_STATE * g:SSM_STATE * (g + 1)]
        cb = _dot_nt(cg, bg)
        for e in range(4 * g, 4 * g + 4):
            hs = slice(HEAD_DIM * e, HEAD_DIM * (e + 1))
            seg = a_cum[:, e:e + 1] - a_cum_t[e:e + 1, :]
            m = (cb * jnp.exp(jnp.where(causal, seg, -1e30))).astype(BF16)
            h_old = hst_ref[e]
            y = _dot(m, xdtb[:, hs]) + _dot_nt(cg, h_old.astype(BF16)) * ea[:, hs]
            y_ref[r0:r0 + q, hs] = y
            decay = jnp.exp(jnp.broadcast_to(a_last[:, e:e + 1], (1, SSM_STATE)))
            hst_ref[e] = decay * h_old + _dot_tn(xw[:, hs], bg)
    y = y_ref[r0:r0 + q, :] + dsk_ref[...] * xs
    y = y * _silu(z_ref[r0:r0 + q, :])
    half = WIDTH // 2
    outs = []
    for g in range(2):
        yg = y[:, half * g:half * (g + 1)]
        outs.append(yg * lax.rsqrt(jnp.mean(yg * yg, axis=-1, keepdims=True) + EPS))
    return jnp.concatenate(outs, axis=1) * gn_ref[...]


def _ssd_body(n_ptiles, tps, seg, q_prompt, xbc_ref, z_ref, dtr_ref, conv0_ref, h0_ref, cw_ref, cb_ref, dtb_ref,
              alog_ref, dsk_ref, gn_ref, exp_ref, tril_ref, eye_ref, o_ref, convp_ref, hp_ref, convs_ref, hs_ref,
              xp_ref, xc_ref, y_ref, hst_ref):
    i = pl.program_id(0)
    tt = xbc_ref.shape[0]
    params = (dtb_ref, alog_ref, dsk_ref, gn_ref, exp_ref, tril_ref, eye_ref)
    tail = slice(SUBLANES - (CONV_W - 1), SUBLANES)

    @pl.when(i < n_ptiles)
    def _():
        @pl.when(i % tps == 0)
        def _():
            xp_ref[0:SUBLANES, :] = jnp.zeros((SUBLANES, SSM_CONV_DIM), F32)
            hst_ref[...] = jnp.zeros_like(hst_ref)

        xp_ref[SUBLANES:SUBLANES + tt, :] = xbc_ref[...]
        xc_ref[...] = _silu(_causal_conv(xp_ref, tt, cw_ref, cb_ref))
        for c in range(tt // q_prompt):
            r0 = c * q_prompt
            o_ref[r0:r0 + q_prompt, :] = _ssd_chunk(xc_ref, z_ref, dtr_ref, y_ref, hst_ref, r0, q_prompt,
                                                    *params).astype(BF16)
        xp_ref[0:SUBLANES, :] = xp_ref[tt:tt + SUBLANES, :]

        @pl.when(i % tps == tps - 1)
        def _():
            convp_ref[0] = xp_ref[tail, :]
            hp_ref[0] = hst_ref[...]

    @pl.when(i >= n_ptiles)
    def _():
        for s in range(tt // seg):
            r0 = s * seg
            xp_ref[tail, :] = conv0_ref[s]
            xp_ref[SUBLANES:SUBLANES + seg, :] = xbc_ref[r0:r0 + seg, :]
            xc_ref[r0:r0 + seg, :] = _silu(_causal_conv(xp_ref, seg, cw_ref, cb_ref))
            hst_ref[...] = h0_ref[s]
            o_ref[r0:r0 + seg, :] = _ssd_chunk(xc_ref, z_ref, dtr_ref, y_ref, hst_ref, r0, seg,
                                               *params).astype(BF16)
            convs_ref[s] = xp_ref[seg + SUBLANES - (CONV_W - 1):seg + SUBLANES, :]
            hs_ref[s] = hst_ref[...]


def _ssd(rest, conv0, h0, cw, cb, dtb, alog, dsk, gn, expand, tril, eye, n_pseq, seq_len, n_sseq, seg):
    t = rest.shape[0]
    tt = TOKEN_TILE
    tps = seq_len // tt
    n_ptiles = n_pseq * tps
    spt = tt // seg
    pseq, stile = _mixer_specs(n_ptiles, tps, spt)
    fixed = lambda i: (0, 0)
    hshape = (N_HEADS, HEAD_DIM, SSM_STATE)
    return pl.pallas_call(
        functools.partial(_ssd_body, n_ptiles, tps, seg, SSD_CHUNK),
        grid=(t // tt,),
        in_specs=[pl.BlockSpec((tt, SSM_CONV_DIM), lambda i: (i, REST_XBC // SSM_CONV_DIM)),
                  pl.BlockSpec((tt, WIDTH), lambda i: (i, REST_ZC // WIDTH)),
                  pl.BlockSpec((tt, LANES), lambda i: (i, REST_DT // LANES)),
                  pl.BlockSpec((spt, CONV_W - 1, SSM_CONV_DIM), lambda i: (stile(i), 0, 0)),
                  pl.BlockSpec((spt,) + hshape, lambda i: (stile(i), 0, 0, 0)),
                  pl.BlockSpec((CONV_W, SSM_CONV_DIM), fixed), pl.BlockSpec((1, SSM_CONV_DIM), fixed),
                  pl.BlockSpec((1, LANES), fixed), pl.BlockSpec((1, LANES), fixed),
                  pl.BlockSpec((1, WIDTH), fixed), pl.BlockSpec((1, WIDTH), fixed),
                  pl.BlockSpec((LANES, WIDTH), fixed), pl.BlockSpec((KEY_BLOCK, KEY_BLOCK), fixed),
                  pl.BlockSpec((SUBLANES, LANES), fixed)],
        out_specs=[pl.BlockSpec((tt, WIDTH), lambda i: (i, 0)),
                   pl.BlockSpec((1, CONV_W - 1, SSM_CONV_DIM), lambda i: (pseq(i), 0, 0)),
                   pl.BlockSpec((1,) + hshape, lambda i: (pseq(i), 0, 0, 0)),
                   pl.BlockSpec((spt, CONV_W - 1, SSM_CONV_DIM), lambda i: (stile(i), 0, 0)),
                   pl.BlockSpec((spt,) + hshape, lambda i: (stile(i), 0, 0, 0))],
        out_shape=[jax.ShapeDtypeStruct((t, WIDTH), BF16),
                   jax.ShapeDtypeStruct((n_pseq, CONV_W - 1, SSM_CONV_DIM), F32),
                   jax.ShapeDtypeStruct((n_pseq,) + hshape, F32),
                   jax.ShapeDtypeStruct((n_sseq, CONV_W - 1, SSM_CONV_DIM), F32),
                   jax.ShapeDtypeStruct((n_sseq,) + hshape, F32)],
        scratch_shapes=[pltpu.VMEM((tt + SUBLANES, SSM_CONV_DIM), F32), pltpu.VMEM((tt, SSM_CONV_DIM), F32),
                        pltpu.VMEM((tt, WIDTH), F32), pltpu.VMEM(hshape, F32)],
        compiler_params=_cparams(("arbitrary",)),
        name="ssd",
    )(rest, rest, rest, conv0, h0, cw, cb, dtb, alog, dsk, gn, expand, tril, eye)


def _route(lt, be_ref, bg_ref):
    n = lt.shape[1]
    le = lt[0:N_EXPERTS, :] + be_ref[:, 0:1]
    lg = lt[N_EXPERTS:N_EXPERTS + SUBLANES, :] + bg_ref[:, 0:1]
    gmax = jnp.max(lg, axis=0, keepdims=True)
    gi = _iota((SUBLANES, n), 0)
    g_sel = jnp.min(jnp.where(lg == gmax, gi, SUBLANES), axis=0, keepdims=True)
    p_sel = 1.0 / jnp.sum(jnp.exp(lg - gmax), axis=0, keepdims=True)
    ei = _iota((N_EXPERTS, n), 0)
    m1 = jnp.where((ei >> 3) == g_sel, le, -jnp.inf)
    v1 = jnp.max(m1, axis=0, keepdims=True)
    i1 = jnp.min(jnp.where(m1 == v1, ei, N_EXPERTS), axis=0, keepdims=True)
    m2 = jnp.where(ei == i1, -jnp.inf, m1)
    v2 = jnp.max(m2, axis=0, keepdims=True)
    i2 = jnp.min(jnp.where(m2 == v2, ei, N_EXPERTS), axis=0, keepdims=True)
    e2 = jnp.exp(v2 - v1)
    w1 = p_sel / (1.0 + e2)
    w2 = w1 * e2
    r = _iota((SUBLANES, n), 0)
    out = jnp.where(r == 0, i1.astype(F32), 0.0)
    out = jnp.where(r == 1, i2.astype(F32), out)
    out = jnp.where(r == 2, w1, out)
    return jnp.where(r == 3, w2, out)


def _merge_body(n_ptiles, h_ref, oap_ref, oas_ref, ob_ref, oc_ref, gmix_ref, wg_ref, bg_ref, wbo_ref, wout_ref,
                gffn_ref, wr_ref, bre_ref, brg_ref, h1_ref, u2_ref, route_ref, ya_ref):
    i = pl.program_id(0)
    x = h_ref[...]
    ub = _rmsnorm(x, gmix_ref[...]).astype(BF16)

    @pl.when(i < n_ptiles)
    def _():
        ya_ref[...] = _dot(oap_ref[...], wbo_ref[0:WIDTH, :])

    @pl.when(i >= n_ptiles)
    def _():
        ya_ref[...] = _dot(oas_ref[...].astype(BF16), wbo_ref[0:WIDTH, :])

    def gate(b):
        return jax.nn.sigmoid(_dot(ub, wg_ref[:, D_MODEL * b:D_MODEL * (b + 1)])
                              + bg_ref[:, D_MODEL * b:D_MODEL * (b + 1)])

    merged = gate(0) * ya_ref[...]
    merged = merged + gate(1) * _dot(ob_ref[...], wbo_ref[WIDTH:2 * WIDTH, :])
    merged = merged + gate(2) * _dot(oc_ref[...], wbo_ref[2 * WIDTH:3 * WIDTH, :])
    h1 = x + _dot(merged.astype(BF16), wout_ref[...])
    h1_ref[...] = h1
    u2 = _rmsnorm(h1, gffn_ref[...])
    u2_ref[...] = u2
    lt = _dot_nt(wr_ref[...].astype(BF16), u2.astype(BF16))
    route_ref[...] = _route(lt, bre_ref, brg_ref)


def _merge(h, oa_p, oa_s, ob, oc, gmix, wg, bg, wbo, wout, gffn, wr, bre, brg):
    t = h.shape[0]
    tm = PROJ_TILE
    n_ptiles = oa_p.shape[0] // tm
    row = lambda i: (i, 0)
    fixed = lambda i: (0, 0)
    full = lambda a: pl.BlockSpec(a.shape, fixed)
    return pl.pallas_call(
        functools.partial(_merge_body, n_ptiles),
        grid=(t // tm,),
        in_specs=[pl.BlockSpec((tm, D_MODEL), row),
                  pl.BlockSpec((tm, WIDTH), lambda i: (jnp.minimum(i, n_ptiles - 1), 0)),
                  pl.BlockSpec((tm, WIDTH), lambda i: (jnp.maximum(i - n_ptiles, 0), 0)),
                  pl.BlockSpec((tm, WIDTH), row), pl.BlockSpec((tm, WIDTH), row),
                  full(gmix), full(wg), full(bg), full(wbo), full(wout), full(gffn), full(wr), full(bre), full(brg)],
        out_specs=[pl.BlockSpec((tm, D_MODEL), row), pl.BlockSpec((tm, D_MODEL), row),
                   pl.BlockSpec((SUBLANES, tm), lambda i: (0, i))],
        out_shape=[jax.ShapeDtypeStruct((t, D_MODEL), F32), jax.ShapeDtypeStruct((t, D_MODEL), F32),
                   jax.ShapeDtypeStruct((SUBLANES, t), F32)],
        scratch_shapes=[pltpu.VMEM((tm, D_MODEL), F32)],
        compiler_params=_cparams(("arbitrary",)),
        name="merge",
    )(h, oa_p, oa_s, ob, oc, gmix, wg, bg, wbo, wout, gffn, wr, bre, brg)


def _rank_body(route_ref, before_ref, ltri_ref, dest_ref, meta_ref, cnt_ref, start_ref):
    p = pl.program_id(0)
    i = pl.program_id(1)
    tm = route_ref.shape[1]
    ei = _iota((N_EXPERTS, tm), 0)
    hot0 = ei == route_ref[0:1, :].astype(I32)
    hot1 = ei == route_ref[1:2, :].astype(I32)
    both = jnp.where(jnp.logical_or(hot0, hot1), 1.0, 0.0)
    tile_cnt = jnp.broadcast_to(jnp.sum(both, axis=1, keepdims=True), (N_EXPERTS, LANES))

    @pl.when(jnp.logical_and(p == 0, i == 0))
    def _():
        cnt_ref[...] = jnp.zeros_like(cnt_ref)

    @pl.when(jnp.logical_and(p == 1, i == 0))
    def _():
        cnt = cnt_ref[...]
        padded = jnp.floor((cnt + (EXPERT_BLOCK - 1)) * (1.0 / EXPERT_BLOCK)) * EXPERT_BLOCK
        start = _sel_left(ltri_ref[...], padded)
        start_ref[...] = start
        lane = _iota((N_EXPERTS, LANES), 1)
        meta_ref[...] = jnp.where(lane == 0, cnt, jnp.where(lane == 1, start, 0.0))
        cnt_ref[...] = jnp.zeros_like(cnt_ref)

    @pl.when(p == 1)
    def _():
        prior = _dot(both.astype(BF16), before_ref[...])
        slot = prior + (start_ref[:, 0:1] + cnt_ref[:, 0:1])
        d0 = jnp.sum(jnp.where(hot0, slot, 0.0), axis=0, keepdims=True)
        d1 = jnp.sum(jnp.where(hot1, slot, 0.0), axis=0, keepdims=True)
        dest_ref[0] = jnp.concatenate([d0, d1], axis=0).astype(I32)

    cnt_ref[...] = cnt_ref[...] + tile_cnt


def _rank(route, before, ltri):
    t = route.shape[1]
    tm = TOKEN_TILE
    nt = t // tm
    return pl.pallas_call(
        _rank_body,
        grid=(2, nt),
        in_specs=[pl.BlockSpec((SUBLANES, tm), lambda p, i: (0, i)),
                  pl.BlockSpec((tm, tm), lambda p, i: (0, 0)),
                  pl.BlockSpec((N_EXPERTS, N_EXPERTS), lambda p, i: (0, 0))],
        out_specs=[pl.BlockSpec((1, 2, tm), lambda p, i: (i * p, 0, 0)),
                   pl.BlockSpec((N_EXPERTS, LANES), lambda p, i: (0, 0))],
        out_shape=[jax.ShapeDtypeStruct((nt, 2, tm), I32), jax.ShapeDtypeStruct((N_EXPERTS, LANES), F32)],
        scratch_shapes=[pltpu.VMEM((N_EXPERTS, LANES), F32), pltpu.VMEM((N_EXPERTS, LANES), F32)],
        compiler_params=_cparams(("arbitrary", "arbitrary")),
        name="moe_rank",
    )(route, before, ltri)


def _row_copy(src_ref, src_row, dst_ref, dst_row, sem):
    return pltpu.make_async_copy(src_ref.at[pl.ds(src_row, 1), :], dst_ref.at[pl.ds(dst_row, 1), :], sem)


def _dispatch_body(dest_ref, u_ref, xs_in_ref, xs_ref, sem):
    del xs_in_ref
    tm = u_ref.shape[0]

    def start(t, c):
        for k in range(2):
            _row_copy(u_ref, t, xs_ref, dest_ref[0, k, t], sem).start()
        return c

    def wait(t, c):
        for k in range(2):
            _row_copy(u_ref, t, xs_ref, dest_ref[0, k, t], sem).wait()
        return c

    lax.fori_loop(0, tm, start, 0, unroll=ROW_DMA_UNROLL)
    lax.fori_loop(0, tm, wait, 0, unroll=ROW_DMA_UNROLL)


def _dispatch(dest, u2, xs_zero):
    t = u2.shape[0]
    tm = dest.shape[2]
    return pl.pallas_call(
        _dispatch_body,
        grid=(t // tm,),
        in_specs=[pl.BlockSpec((1, 2, tm), lambda i: (i, 0, 0), memory_space=pltpu.SMEM),
                  pl.BlockSpec((tm, D_MODEL), lambda i: (i, 0)),
                  pl.BlockSpec(memory_space=pl.ANY)],
        out_specs=pl.BlockSpec(memory_space=pl.ANY),
        out_shape=jax.ShapeDtypeStruct(xs_zero.shape, F32),
        scratch_shapes=[pltpu.SemaphoreType.DMA(())],
        input_output_aliases={2: 0},
        compiler_params=_cparams(("arbitrary",), disable_bounds_checks=True),
        name="moe_dispatch",
    )(dest, u2, xs_zero)


def _experts_body(be_ref, nu_ref, x_ref, wg_ref, wu_ref, wd_ref, y_ref, wgb_ref, wub_ref, wdb_ref):
    i = pl.program_id(0)
    live = i < nu_ref[0]

    @pl.when(jnp.logical_or(i == 0, be_ref[i] != be_ref[jnp.maximum(i - 1, 0)]))
    def _():
        wgb_ref[...] = wg_ref[0, 0].astype(BF16)
        wub_ref[...] = wu_ref[0, 0].astype(BF16)
        wdb_ref[...] = wd_ref[0, 0].astype(BF16)

    @pl.when(live)
    def _():
        xb = x_ref[...].astype(BF16)
        a = _silu(_dot(xb, wgb_ref[...])) * _dot(xb, wub_ref[...])
        y_ref[...] = _dot(a.astype(BF16), wdb_ref[...])

    @pl.when(jnp.logical_not(live))
    def _():
        y_ref[...] = jnp.zeros_like(y_ref)


def _experts(blk_expert, n_used, xs, wg, wu, wd, layer):
    cap = xs.shape[0]
    nb = cap // EXPERT_BLOCK
    live = lambda i, nu: jnp.minimum(i, nu[0] - 1)
    rows = lambda i, be, nu: (live(i, nu), 0)
    wsel = lambda i, be, nu: (layer, be[live(i, nu)], 0, 0)
    return pl.pallas_call(
        _experts_body,
        grid_spec=pltpu.PrefetchScalarGridSpec(
            num_scalar_prefetch=2,
            grid=(nb,),
            in_specs=[pl.BlockSpec((EXPERT_BLOCK, D_MODEL), rows),
                      pl.BlockSpec((1, 1, D_MODEL, D_EXPERT), wsel), pl.BlockSpec((1, 1, D_MODEL, D_EXPERT), wsel),
                      pl.BlockSpec((1, 1, D_EXPERT, D_MODEL), wsel)],
            out_specs=pl.BlockSpec((EXPERT_BLOCK, D_MODEL), lambda i, be, nu: (i, 0)),
            scratch_shapes=[pltpu.VMEM((D_MODEL, D_EXPERT), BF16), pltpu.VMEM((D_MODEL, D_EXPERT), BF16),
                            pltpu.VMEM((D_EXPERT, D_MODEL), BF16)]),
        out_shape=jax.ShapeDtypeStruct((cap, D_MODEL), F32),
        compiler_params=_cparams(("arbitrary",)),
        name="moe_experts",
    )(blk_expert, n_used, xs, wg, wu, wd)


def _combine_body(n_ptiles, per, dest_ref, h_ref, route_ref, g_ref, y_hbm, *rest):
    *o_refs, ybuf_ref, sem = rest
    i = pl.program_id(0)
    tm = h_ref.shape[0]
    base = (i % per) * tm

    def start(t, c):
        for k in range(2):
            _row_copy(y_hbm, dest_ref[0, k, base + t], ybuf_ref.at[k], t, sem).start()
        return c

    def wait(t, c):
        for k in range(2):
            _row_copy(y_hbm, dest_ref[0, k, base + t], ybuf_ref.at[k], t, sem).wait()
        return c

    lax.fori_loop(0, tm, start, 0, unroll=ROW_DMA_UNROLL)
    eye = _iota((tm, tm), 0) == _iota((tm, tm), 1)
    w0 = jnp.sum(jnp.where(eye, route_ref[2:3, :], 0.0), axis=1, keepdims=True)
    w1 = jnp.sum(jnp.where(eye, route_ref[3:4, :], 0.0), axis=1, keepdims=True)
    lax.fori_loop(0, tm, wait, 0, unroll=ROW_DMA_UNROLL)
    h2 = h_ref[...] + (w0 * ybuf_ref[0] + w1 * ybuf_ref[1])
    if len(o_refs) == 1:
        o_refs[0][...] = h2
    else:
        @pl.when(i < n_ptiles)
        def _():
            o_refs[0][...] = _rmsnorm(h2, g_ref[...])

        @pl.when(i >= n_ptiles)
        def _():
            o_refs[1][...] = _rmsnorm(h2, g_ref[...])


def _combine(dest, h1, route, g_final, y, t_p, final):
    t = h1.shape[0]
    tm = PROJ_TILE
    per = dest.shape[2] // tm
    n_ptiles = t_p // tm
    if final:
        out_specs = [pl.BlockSpec((tm, D_MODEL), lambda i: (jnp.minimum(i, n_ptiles - 1), 0)),
                     pl.BlockSpec((tm, D_MODEL), lambda i: (jnp.maximum(i - n_ptiles, 0), 0))]
        out_shape = [jax.ShapeDtypeStruct((t_p, D_MODEL), F32), jax.ShapeDtypeStruct((t - t_p, D_MODEL), F32)]
    else:
        out_specs = pl.BlockSpec((tm, D_MODEL), lambda i: (i, 0))
        out_shape = jax.ShapeDtypeStruct((t, D_MODEL), F32)
    return pl.pallas_call(
        functools.partial(_combine_body, n_ptiles, per),
        grid=(t // tm,),
        in_specs=[pl.BlockSpec((1, 2, dest.shape[2]), lambda i: (i // per, 0, 0), memory_space=pltpu.SMEM),
                  pl.BlockSpec((tm, D_MODEL), lambda i: (i, 0)),
                  pl.BlockSpec((SUBLANES, tm), lambda i: (0, i)),
                  pl.BlockSpec((1, D_MODEL), lambda i: (0, 0)),
                  pl.BlockSpec(memory_space=pl.ANY)],
        out_specs=out_specs,
        out_shape=out_shape,
        scratch_shapes=[pltpu.VMEM((2, tm, D_MODEL), F32), pltpu.SemaphoreType.DMA(())],
        compiler_params=_cparams(("arbitrary",), disable_bounds_checks=True),
        name="moe_combine",
    )(dest, h1, route, g_final, y)


def _constants():
    r = jnp.arange(3 * KEY_BLOCK)
    c = jnp.arange(2 * KEY_BLOCK)
    tri = jnp.where(c[None, :] < KEY_BLOCK, (r[:, None] % KEY_BLOCK) > c[None, :], True).astype(BF16)
    q = jnp.arange(KEY_BLOCK)
    tril = (q[:, None] >= q[None, :]).astype(BF16)
    lane = jnp.arange(WIDTH)
    expand = (jnp.arange(LANES)[:, None] == lane[None, :] // HEAD_DIM).astype(BF16)
    eye = (jnp.arange(SUBLANES)[:, None] == jnp.arange(LANES)[None, :]).astype(BF16)
    t = jnp.arange(TOKEN_TILE)
    before = (t[:, None] < t[None, :]).astype(BF16)
    e = jnp.arange(N_EXPERTS)
    ltri = (e[:, None] > e[None, :]).astype(BF16)
    return tri, tril, expand, eye, before, ltri


def _block_diag(w):
    n, d = w.shape[0], w.shape[1]
    out = jnp.zeros((n, d, n, d), w.dtype)
    out = out.at[jnp.arange(n), :, jnp.arange(n), :].set(w)
    return out.reshape(n * d, n * d)


def _proj_weight(w_in):
    wq = w_in[:, 0:WIDTH].reshape(D_MODEL, N_HEADS, 1, HEAD_DIM)
    half = (jnp.arange(N_HEADS) % 2)[None, :, None, None] == jnp.arange(2)[None, None, :, None]
    wqm = jnp.where(half, wq, 0.0).reshape(D_MODEL, QM_W)
    o = 3 * WIDTH
    xb, gb, zc = w_in[:, o:o + WIDTH], w_in[:, o + WIDTH:o + 2 * WIDTH], w_in[:, o + 2 * WIDTH:o + 3 * WIDTH]
    xbc = w_in[:, o + 3 * WIDTH:o + 3 * WIDTH + SSM_CONV_DIM]
    dt = jnp.pad(w_in[:, o + 3 * WIDTH + SSM_CONV_DIM:], ((0, 0), (0, LANES - N_HEADS)))
    return jnp.concatenate([wqm, w_in[:, WIDTH:3 * WIDTH], xbc, xb, gb, zc, dt], axis=1).astype(BF16)


def _pad_lanes(v):
    return jnp.pad(v, (0, LANES - v.shape[0]))[None, :]


def kernel(x_prompt, x_sample, cache_sb_k, cache_sb_v, state_lru_conv, state_lru_h, state_ssm_conv, state_ssm_h, g_mix, w_in, w_gate, b_gate, w_branch_out, w_out, lru_conv_w, lru_conv_b, lru_w_a, lru_b_a, lru_w_x, lru_b_x, lru_lambda, ssm_conv_w, ssm_conv_b, ssm_dt_bias, ssm_a_log, ssm_d, ssm_norm_g, g_ffn, w_router_group, b_router_group, w_router_expert, b_router_expert, w_expert_gate, w_expert_up, w_expert_down, g_final):
    n_p, s_p, _ = x_prompt.shape
    n_s, s_s, _ = x_sample.shape
    depth = w_in.shape[0]
    past = cache_sb_k.shape[2]
    t_p, t_s = n_p * s_p, n_s * s_s
    t = t_p + t_s
    assert s_p % TOKEN_TILE == 0 and TOKEN_TILE % s_s == 0 and t_s % TOKEN_TILE == 0
    assert s_s % SUBLANES == 0 and past % SAMPLE_WINDOW == 0

    tri, tril, expand, eye, before, ltri = _constants()
    cache_k = cache_sb_k.reshape(depth, n_s, past * N_HEADS, HEAD_DIM)
    cache_v = cache_sb_v.reshape(depth, n_s, past * N_HEADS, HEAD_DIM)
    n_blocks = -(-2 * t // EXPERT_BLOCK) + N_EXPERTS
    xs_zero = jnp.zeros((n_blocks * EXPERT_BLOCK, D_MODEL), F32)

    h = jnp.concatenate([x_prompt.reshape(t_p, D_MODEL), x_sample.reshape(t_s, D_MODEL)], axis=0)
    states = []
    for l in range(depth):
        row = lambda a: a[l][None, :]
        qm, k_p, v_p, k_s, v_s, kb, vb, rest = _inproj(h, row(g_mix), _proj_weight(w_in[l]), t_p)

        oa_p = _attn_prompt(qm, kb, vb, tri, n_p, s_p)
        acc, rp = _attn_sample(qm, kb, vb, cache_k, cache_v, tri, l, t_p, n_s, s_s)
        if past > SAMPLE_WINDOW:
            older = functools.partial(_attn_older, qm, cache_k, cache_v, tri=tri, layer=l, row0=t_p, n_seq=n_s,
                                      tq=s_s)
            oa_s = lax.cond(jnp.max(rp[:, :N_HEADS]) > LOG_CUTOFF,
                            lambda a, r: older(acc=a, rp=r), lambda a, r: a, acc, rp)
        else:
            oa_s = acc

        ob, lconv_p, lh_p, lconv_s, lh_s = _lru(
            rest, state_lru_conv[l], state_lru_h[l], lru_conv_w[l], row(lru_conv_b),
            _block_diag(lru_w_a[l]).astype(BF16), row(lru_b_a), _block_diag(lru_w_x[l]).astype(BF16),
            row(lru_b_x), row(lru_lambda), n_p, s_p, n_s, s_s)

        oc, sconv_p, sh_p, sconv_s, sh_s = _ssd(
            rest, state_ssm_conv[l], state_ssm_h[l], ssm_conv_w[l], row(ssm_conv_b),
            _pad_lanes(ssm_dt_bias[l]), _pad_lanes(ssm_a_log[l]), jnp.repeat(ssm_d[l], HEAD_DIM)[None, :],
            row(ssm_norm_g), expand, tril, eye, n_p, s_p, n_s, s_s)

        w_r = jnp.concatenate([w_router_expert[l].T, w_router_group[l].T,
                               jnp.zeros((SUBLANES - N_GROUPS, D_MODEL), F32)], axis=0)
        b_re = jnp.broadcast_to(b_router_expert[l][:, None], (N_EXPERTS, LANES))
        b_rg = jnp.broadcast_to(jnp.concatenate([b_router_group[l], jnp.full((SUBLANES - N_GROUPS,), -1e30, F32)])[:, None],
                                (SUBLANES, LANES))
        h1, u2, route = _merge(h, oa_p, oa_s, ob, oc, row(g_mix), w_gate[l].astype(BF16), row(b_gate),
                               w_branch_out[l].astype(BF16), w_out[l].astype(BF16), row(g_ffn), w_r, b_re, b_rg)

        dest, meta = _rank(route, before, ltri)
        counts, starts = meta[:, 0], meta[:, 1]
        ends = starts + jnp.ceil(counts / EXPERT_BLOCK) * EXPERT_BLOCK
        blk_row = (jnp.arange(n_blocks) * EXPERT_BLOCK).astype(F32)
        blk_expert = jnp.minimum(jnp.sum(ends[None, :] <= blk_row[:, None], axis=1), N_EXPERTS - 1).astype(I32)
        n_used = (ends[N_EXPERTS - 1:] / EXPERT_BLOCK).astype(I32)

        xs = _dispatch(dest, u2, xs_zero)
        y = _experts(blk_expert, n_used, xs, w_expert_gate, w_expert_up, w_expert_down, l)
        h = _combine(dest, h1, route, g_final[None, :], y, t_p, final=(l == depth - 1))

        head = lambda a, n, s: a.reshape(n, s, N_HEADS, HEAD_DIM)
        states.append((head(k_p, n_p, s_p), head(v_p, n_p, s_p), lconv_p, lh_p[:, 0], sconv_p, sh_p,
                       head(k_s, n_s, s_s), head(v_s, n_s, s_s), lconv_s, lh_s, sconv_s, sh_s))

    stacked = tuple(jnp.stack([st[j] for st in states], axis=0) for j in range(12))
    y_p, y_s = h
    return (y_p.reshape(n_p, s_p, D_MODEL), y_s.reshape(n_s, s_s, D_MODEL)) + stacked
```

```python
import functools

import jax
import jax.numpy as jnp
from jax import lax
from jax.experimental import pallas as pl
from jax.experimental.pallas import tpu as pltpu

F32 = jnp.float32
BF16 = jnp.bfloat16
I32 = jnp.int32

D_MODEL = 1024
N_HEADS = 8
HEAD_DIM = 64
WIDTH = 512
CONV_W = 4
SSM_STATE = 128
SSM_CONV_DIM = 1024
N_GROUPS = 4
PER_GROUP = 8
N_EXPERTS = 32
D_EXPERT = 512
LRU_C = 8.0
EPS = 1e-6

LANES = 128
SUBLANES = 8
TOKEN_TILE = 512
PROJ_TILE = 256
KEY_BLOCK = 128
SSD_CHUNK = 64
SAMPLE_WINDOW = 512
EXPERT_BLOCK = 256
VMEM_LIMIT = 56 * 1024 * 1024

LOG_CUTOFF = -88.0

REST_XBC, REST_XB, REST_GB, REST_ZC, REST_DT = 0, 1024, 1536, 2048, 2560
REST_W = 2688
QM_W = 1024
PROJ_W = QM_W + 2 * WIDTH + REST_W


def _cparams(sem, **kw):
    return pltpu.CompilerParams(dimension_semantics=sem, vmem_limit_bytes=VMEM_LIMIT, **kw)


ROW_DMA_UNROLL = 8


def _split3(x):
    hi = x.astype(BF16)
    r = x - hi.astype(F32)
    mid = r.astype(BF16)
    lo = (r - mid.astype(F32)).astype(BF16)
    return hi, mid, lo


def _dot(a, b):
    return jnp.dot(a, b, preferred_element_type=F32)


def _dot_nt(a, b):
    return lax.dot_general(a, b, (((1,), (1,)), ((), ())), preferred_element_type=F32)


def _dot_tn(a, b):
    return lax.dot_general(a, b, (((0,), (0,)), ((), ())), preferred_element_type=F32)


def _sel_right(x, m01):
    return sum(_dot(p, m01) for p in _split3(x))


def _sel_left(m01, x):
    return sum(_dot(m01, p) for p in _split3(x))


def _rmsnorm(x, g):
    return x * lax.rsqrt(jnp.mean(x * x, axis=-1, keepdims=True) + EPS) * g


def _log_sigmoid(z):
    return jnp.minimum(z, 0.0) - jnp.log1p(jnp.exp(-jnp.abs(z)))


def _silu(x):
    return x * jax.nn.sigmoid(x)


def _iota(shape, dim):
    return lax.broadcasted_iota(I32, shape, dim)


def _store_heads(ref, x):
    rows = x.shape[0]
    for h in range(N_HEADS):
        ref[pl.ds(h, rows, stride=N_HEADS), :] = x[:, HEAD_DIM * h:HEAD_DIM * (h + 1)]


def _inproj_body(n_ptiles, h_ref, g_ref, w_ref, qm_ref, kp_ref, vp_ref, ks_ref, vs_ref, kb_ref, vb_ref, rest_ref):
    i = pl.program_id(0)
    ub = _rmsnorm(h_ref[...], g_ref[...]).astype(BF16)

    def proj(c0, c1):
        return _dot(ub, w_ref[:, c0:c1])

    for c in range(0, QM_W, 512):
        qm_ref[:, c:c + 512] = proj(c, c + 512).astype(BF16)
    k = proj(QM_W, QM_W + WIDTH)
    kb_ref[...] = k.astype(BF16)
    v = proj(QM_W + WIDTH, QM_W + 2 * WIDTH)
    vb_ref[...] = v.astype(BF16)

    @pl.when(i < n_ptiles)
    def _():
        kp_ref[0] = k.T
        vp_ref[0] = v.T

    @pl.when(i >= n_ptiles)
    def _():
        _store_heads(ks_ref, k)
        _store_heads(vs_ref, v)

    base = QM_W + 2 * WIDTH
    for c in range(0, REST_W, 512):
        c1 = min(c + 512, REST_W)
        rest_ref[:, c:c1] = proj(base + c, base + c1)


def _inproj(h, g, w, n_p, s_p):
    t = h.shape[0]
    tm = PROJ_TILE
    t_p = n_p * s_p
    n_ptiles = t_p // tm
    row = lambda i: (i, 0)
    fixed = lambda i: (0, 0)
    tps = s_p // tm
    ptile = lambda i: jnp.minimum(i, n_ptiles - 1)
    prow = pl.BlockSpec((1, WIDTH, tm), lambda i: (ptile(i) // tps, 0, ptile(i) % tps))
    pshape = jax.ShapeDtypeStruct((n_p, WIDTH, s_p), F32)
    srow = pl.BlockSpec((tm * N_HEADS, HEAD_DIM), lambda i: (jnp.maximum(i - n_ptiles, 0), 0))
    heads = lambda n: jax.ShapeDtypeStruct((n * N_HEADS, HEAD_DIM), F32)
    return pl.pallas_call(
        functools.partial(_inproj_body, n_ptiles),
        grid=(t // tm,),
        in_specs=[pl.BlockSpec((tm, D_MODEL), row), pl.BlockSpec((1, D_MODEL), fixed),
                  pl.BlockSpec((D_MODEL, PROJ_W), fixed)],
        out_specs=[pl.BlockSpec((tm, QM_W), row), prow, prow, srow, srow, pl.BlockSpec((tm, WIDTH), row),
                   pl.BlockSpec((tm, WIDTH), row), pl.BlockSpec((tm, REST_W), row)],
        out_shape=[jax.ShapeDtypeStruct((t, QM_W), BF16), pshape, pshape, heads(t - t_p), heads(t - t_p),
                   jax.ShapeDtypeStruct((t, WIDTH), BF16), jax.ShapeDtypeStruct((t, WIDTH), BF16),
                   jax.ShapeDtypeStruct((t, REST_W), F32)],
        compiler_params=_cparams(("arbitrary",)),
        name="inproj",
    )(h, g, w)


def _sb_weights(zs, mask, r_ref, tri_ref):
    kb = zs[0].shape[1]
    log_beta, log_keep = [], []
    for z in zs:
        ls = _log_sigmoid(z)
        lk = ls - z
        log_beta.append(ls)
        log_keep.append(lk if mask is None else jnp.where(mask, lk, 0.0))
    sums = []
    for lk in log_keep:
        if kb == KEY_BLOCK:
            res = _dot(jnp.concatenate(_split3(lk), axis=1), tri_ref[...])
            sums.append((res[:, :KEY_BLOCK], res[:, KEY_BLOCK:]))
        else:
            sums.append((_sel_right(lk, tri_ref[0:kb, 0:kb]),
                         _sel_right(lk, tri_ref[0:kb, KEY_BLOCK:2 * KEY_BLOCK])))
    ws, top = [], None
    for h, (ls, (la, tot)) in enumerate(zip(log_beta, sums)):
        r_old = r_ref[h]
        w = jnp.exp(ls + la + r_old[:, :kb])
        ws.append((w if mask is None else jnp.where(mask, w, 0.0)).astype(BF16))
        r_new = r_old + tot
        r_ref[h] = r_new
        top = r_new if top is None else jnp.maximum(top, r_new)
    return ws, jnp.max(top)


def _sb_block_pairs(qm, kblk, vblk, mask, acc_ref, r_ref, tri_ref):
    pair = lambda x, h: x[:, LANES * (h // 2):LANES * (h // 2 + 1)]
    zs = [_dot_nt(qm[:, LANES * h:LANES * (h + 1)], pair(kblk, h)) * (HEAD_DIM ** -0.5) for h in range(N_HEADS)]
    ws, rmax = _sb_weights(zs, mask, r_ref, tri_ref)
    low = _iota((qm.shape[0], LANES), 1) < HEAD_DIM
    for p in range(N_HEADS // 2):
        v2 = pair(vblk, 2 * p)
        acc_ref[:, LANES * p:LANES * (p + 1)] += jnp.where(low, _dot(ws[2 * p], v2), _dot(ws[2 * p + 1], v2))
    return rmax


def _sb_block_heads(q_heads, k_of, v_of, mask, acc_ref, r_ref, tri_ref, keys_on_lanes=False):
    qk, wv = (_dot, _dot_nt) if keys_on_lanes else (_dot_nt, _dot)
    zs = [qk(q_heads[h], k_of(h)) * (HEAD_DIM ** -0.5) for h in range(N_HEADS)]
    ws, rmax = _sb_weights(zs, mask, r_ref, tri_ref)
    for h in range(N_HEADS):
        acc_ref[:, HEAD_DIM * h:HEAD_DIM * (h + 1)] += wv(ws[h], v_of(h))
    return rmax


def _query_heads(qm):
    return [qm[:, LANES * h + HEAD_DIM * (h % 2):LANES * h + HEAD_DIM * (h % 2 + 1)] for h in range(N_HEADS)]


def _cache_heads(c_ref, block):
    return lambda h: c_ref[0, 0, h, :, KEY_BLOCK * block:KEY_BLOCK * (block + 1)].astype(BF16)


def _more_keys(c):
    j, rmax = c
    return jnp.logical_and(j >= 0, rmax > LOG_CUTOFF)


def _attn_prompt_body(qm_ref, kb_ref, vb_ref, tri_ref, o_ref, acc_ref, r_ref):
    i = pl.program_id(1)
    tq = qm_ref.shape[0]
    acc_ref[...] = jnp.zeros_like(acc_ref)
    r_ref[...] = jnp.zeros_like(r_ref)
    qm = qm_ref[...]

    def block(j, mask):
        off = pl.multiple_of(j * KEY_BLOCK, KEY_BLOCK)
        return _sb_block_pairs(qm, kb_ref[pl.ds(off, KEY_BLOCK), :], vb_ref[pl.ds(off, KEY_BLOCK), :],
                               mask, acc_ref, r_ref, tri_ref)

    rmax = block(i, _iota((tq, tq), 1) < _iota((tq, tq), 0))
    lax.while_loop(_more_keys, lambda c: (c[0] - 1, block(c[0], None)), (i - 1, rmax))
    o_ref[...] = acc_ref[...].astype(BF16)


def _attn_prompt(qm, kb, vb, tri, n_seq, seq_len):
    tq = KEY_BLOCK
    nq = seq_len // tq
    return pl.pallas_call(
        _attn_prompt_body,
        grid=(n_seq, nq),
        in_specs=[pl.BlockSpec((tq, QM_W), lambda b, i: (b * nq + i, 0)),
                  pl.BlockSpec((seq_len, WIDTH), lambda b, i: (b, 0)),
                  pl.BlockSpec((seq_len, WIDTH), lambda b, i: (b, 0)),
                  pl.BlockSpec((3 * KEY_BLOCK, 2 * KEY_BLOCK), lambda b, i: (0, 0))],
        out_specs=pl.BlockSpec((tq, WIDTH), lambda b, i: (b * nq + i, 0)),
        out_shape=jax.ShapeDtypeStruct((n_seq * seq_len, WIDTH), BF16),
        scratch_shapes=[pltpu.VMEM((tq, WIDTH), F32), pltpu.VMEM((N_HEADS, tq, LANES), F32)],
        compiler_params=_cparams(("arbitrary", "arbitrary")),
        name="attn_prompt",
    )(qm, kb, vb, tri)


def _pack_r(r_ref, tq):
    lane = _iota((tq, LANES), 1)
    rp = jnp.zeros((tq, LANES), F32)
    for h in range(N_HEADS):
        rp = jnp.where(lane == h, r_ref[h], rp)
    return rp


def _attn_sample_body(qm_ref, kn_ref, vn_ref, ck_ref, cv_ref, tri_ref, acc_out, rp_out, acc_ref, r_ref):
    tq = qm_ref.shape[0]
    acc_ref[...] = jnp.zeros_like(acc_ref)
    r_ref[...] = jnp.zeros_like(r_ref)
    q_heads = _query_heads(qm_ref[...])
    kn, vn = kn_ref[...], vn_ref[...]
    head = lambda x: (lambda h: x[:, HEAD_DIM * h:HEAD_DIM * (h + 1)])
    causal = _iota((tq, tq), 1) < _iota((tq, tq), 0)
    rmax = _sb_block_heads(q_heads, head(kn), head(vn), causal, acc_ref, r_ref, tri_ref)

    for j in reversed(range(ck_ref.shape[4] // KEY_BLOCK)):
        rmax = lax.cond(rmax > LOG_CUTOFF,
                        lambda j=j: _sb_block_heads(q_heads, _cache_heads(ck_ref, j), _cache_heads(cv_ref, j), None,
                                                    acc_ref, r_ref, tri_ref, keys_on_lanes=True),
                        lambda rmax=rmax: rmax)
    acc_out[...] = acc_ref[...]
    rp_out[...] = _pack_r(r_ref, tq)


def _attn_sample(qm, kb, vb, cache_k, cache_v, tri, layer, row0, n_seq, tq):
    past = cache_k.shape[4]
    win = min(SAMPLE_WINDOW, past)
    blk0 = row0 // tq
    cur = lambda b: (blk0 + b, 0)
    cblock = pl.BlockSpec((1, 1, N_HEADS, HEAD_DIM, win), lambda b: (layer, b, 0, 0, past // win - 1))
    return pl.pallas_call(
        _attn_sample_body,
        grid=(n_seq,),
        in_specs=[pl.BlockSpec((tq, QM_W), cur), pl.BlockSpec((tq, WIDTH), cur), pl.BlockSpec((tq, WIDTH), cur),
                  cblock, cblock,
                  pl.BlockSpec((3 * KEY_BLOCK, 2 * KEY_BLOCK), lambda b: (0, 0))],
        out_specs=[pl.BlockSpec((tq, WIDTH), lambda b: (b, 0)), pl.BlockSpec((tq, LANES), lambda b: (b, 0))],
        out_shape=[jax.ShapeDtypeStruct((n_seq * tq, WIDTH), F32), jax.ShapeDtypeStruct((n_seq * tq, LANES), F32)],
        scratch_shapes=[pltpu.VMEM((tq, WIDTH), F32), pltpu.VMEM((N_HEADS, tq, LANES), F32)],
        compiler_params=_cparams(("arbitrary",)),
        name="attn_sample",
    )(qm, kb, vb, cache_k, cache_v, tri)


def _attn_older_body(qm_ref, ck_ref, cv_ref, acc_in, rp_in, tri_ref, acc_out, acc_ref, r_ref):
    s = pl.program_id(1)
    tq = qm_ref.shape[0]

    @pl.when(s == 0)
    def _():
        acc_ref[...] = acc_in[...]
        rp = rp_in[...]
        for h in range(N_HEADS):
            r_ref[h] = jnp.broadcast_to(rp[:, h:h + 1], (tq, LANES))

    rmax = jnp.max(r_ref[0])
    for h in range(1, N_HEADS):
        rmax = jnp.maximum(rmax, jnp.max(r_ref[h]))

    @pl.when(rmax > LOG_CUTOFF)
    def _():
        _sb_block_heads(_query_heads(qm_ref[...]), _cache_heads(ck_ref, 0), _cache_heads(cv_ref, 0), None,
                        acc_ref, r_ref, tri_ref, keys_on_lanes=True)

    @pl.when(s == pl.num_programs(1) - 1)
    def _():
        acc_out[...] = acc_ref[...]


def _attn_older(qm, cache_k, cache_v, acc, rp, tri, layer, row0, n_seq, tq):
    past = cache_k.shape[4]
    win = min(SAMPLE_WINDOW, past)
    nb = (past - win) // KEY_BLOCK
    blk0 = row0 // tq
    cblock = pl.BlockSpec((1, 1, N_HEADS, HEAD_DIM, KEY_BLOCK), lambda b, s: (layer, b, 0, 0, nb - 1 - s))
    return pl.pallas_call(
        _attn_older_body,
        grid=(n_seq, nb),
        in_specs=[pl.BlockSpec((tq, QM_W), lambda b, s: (blk0 + b, 0)), cblock, cblock,
                  pl.BlockSpec((tq, WIDTH), lambda b, s: (b, 0)), pl.BlockSpec((tq, LANES), lambda b, s: (b, 0)),
                  pl.BlockSpec((3 * KEY_BLOCK, 2 * KEY_BLOCK), lambda b, s: (0, 0))],
        out_specs=pl.BlockSpec((tq, WIDTH), lambda b, s: (b, 0)),
        out_shape=jax.ShapeDtypeStruct((n_seq * tq, WIDTH), F32),
        scratch_shapes=[pltpu.VMEM((tq, WIDTH), F32), pltpu.VMEM((N_HEADS, tq, LANES), F32)],
        compiler_params=_cparams(("arbitrary", "arbitrary")),
        name="attn_older",
    )(qm, cache_k, cache_v, acc, rp, tri)


def _causal_conv(xp_ref, n, w_ref, b_ref):
    y = b_ref[...]
    for k in range(CONV_W):
        y = y + xp_ref[pl.ds(SUBLANES - (CONV_W - 1) + k, n), :] * w_ref[k:k + 1, :]
    return y


def _lru_rows(xp_ref, a_ref, b_ref, n, h0, gate, cw_ref, cb_ref, wa_ref, ba_ref, wx_ref, bx_ref, lam_ref):
    xc = _causal_conv(xp_ref, n, cw_ref, cb_ref)
    xcb = xc.astype(BF16)
    r = jax.nn.sigmoid(_dot(xcb, wa_ref[...]) + ba_ref[...])
    ig = jax.nn.sigmoid(_dot(xcb, wx_ref[...]) + bx_ref[...])
    log_a = LRU_C * r * _log_sigmoid(lam_ref[...])
    a_ref[0:n, :] = jnp.exp(log_a)
    th = jnp.tanh(log_a)
    b_ref[0:n, :] = jnp.sqrt(-2.0 * th / (1.0 - th)) * (ig * xc)
    row = _iota((SUBLANES, WIDTH), 0)

    def group(g, h):
        off = pl.multiple_of(g * SUBLANES, SUBLANES)
        a = a_ref[pl.ds(off, SUBLANES), :]
        b = b_ref[pl.ds(off, SUBLANES), :]
        for s in (1, 2, 4):
            a_prev = jnp.where(row >= s, pltpu.roll(a, s, 0), 1.0)
            b_prev = jnp.where(row >= s, pltpu.roll(b, s, 0), 0.0)
            b = b + a * b_prev
            a = a * a_prev
        hs = a * h + b
        b_ref[pl.ds(off, SUBLANES), :] = hs
        return hs[SUBLANES - 1:SUBLANES, :]

    h_last = lax.fori_loop(0, n // SUBLANES, group, h0)
    y = b_ref[0:n, :] * jax.nn.gelu(gate, approximate=True)
    return y, h_last


def _lru_body(n_ptiles, tps, seg, xb_ref, gb_ref, conv0_ref, h0_ref, cw_ref, cb_ref, wa_ref, ba_ref, wx_ref,
              bx_ref, lam_ref, o_ref, convp_ref, hp_ref, convs_ref, hs_ref, xp_ref, a_ref, b_ref, hc_ref):
    i = pl.program_id(0)
    tt = xb_ref.shape[0]
    params = (cw_ref, cb_ref, wa_ref, ba_ref, wx_ref, bx_ref, lam_ref)

    @pl.when(i < n_ptiles)
    def _():
        @pl.when(i % tps == 0)
        def _():
            xp_ref[0:SUBLANES, :] = jnp.zeros((SUBLANES, WIDTH), F32)
            hc_ref[...] = jnp.zeros_like(hc_ref)

        xp_ref[SUBLANES:SUBLANES + tt, :] = xb_ref[...]
        y, h_last = _lru_rows(xp_ref, a_ref, b_ref, tt, hc_ref[...], gb_ref[...], *params)
        o_ref[...] = y.astype(BF16)
        hc_ref[...] = h_last
        xp_ref[0:SUBLANES, :] = xp_ref[tt:tt + SUBLANES, :]

        @pl.when(i % tps == tps - 1)
        def _():
            convp_ref[0] = xp_ref[SUBLANES - (CONV_W - 1):SUBLANES, :]
            hp_ref[0] = h_last

    @pl.when(i >= n_ptiles)
    def _():
        for s in range(tt // seg):
            xp_ref[SUBLANES - (CONV_W - 1):SUBLANES, :] = conv0_ref[s]
            xp_ref[SUBLANES:SUBLANES + seg, :] = xb_ref[s * seg:(s + 1) * seg, :]
            y, h_last = _lru_rows(xp_ref, a_ref, b_ref, seg, h0_ref[s:s + 1, :],
                                  gb_ref[s * seg:(s + 1) * seg, :], *params)
            o_ref[s * seg:(s + 1) * seg, :] = y.astype(BF16)
            convs_ref[s] = xp_ref[seg + SUBLANES - (CONV_W - 1):seg + SUBLANES, :]
            hs_ref[s:s + 1, :] = h_last


def _mixer_specs(n_ptiles, tps, spt):
    pseq = lambda i: jnp.minimum(i, n_ptiles - 1) // tps
    stile = lambda i: jnp.maximum(i - n_ptiles, 0)
    return pseq, stile


def _lru(rest, conv0, h0, cw, cb, wa, ba, wx, bx, lam, n_pseq, seq_len, n_sseq, seg):
    t = rest.shape[0]
    tt = TOKEN_TILE
    tps = seq_len // tt
    n_ptiles = n_pseq * tps
    spt = tt // seg
    pseq, stile = _mixer_specs(n_ptiles, tps, spt)
    fixed = lambda i: (0, 0)
    vec = pl.BlockSpec((1, WIDTH), fixed)
    return pl.pallas_call(
        functools.partial(_lru_body, n_ptiles, tps, seg),
        grid=(t // tt,),
        in_specs=[pl.BlockSpec((tt, WIDTH), lambda i: (i, REST_XB // WIDTH)),
                  pl.BlockSpec((tt, WIDTH), lambda i: (i, REST_GB // WIDTH)),
                  pl.BlockSpec((spt, CONV_W - 1, WIDTH), lambda i: (stile(i), 0, 0)),
                  pl.BlockSpec((spt, WIDTH), lambda i: (stile(i), 0)),
                  pl.BlockSpec((CONV_W, WIDTH), fixed), vec,
                  pl.BlockSpec((WIDTH, WIDTH), fixed), vec, pl.BlockSpec((WIDTH, WIDTH), fixed), vec, vec],
        out_specs=[pl.BlockSpec((tt, WIDTH), lambda i: (i, 0)),
                   pl.BlockSpec((1, CONV_W - 1, WIDTH), lambda i: (pseq(i), 0, 0)),
                   pl.BlockSpec((1, 1, WIDTH), lambda i: (pseq(i), 0, 0)),
                   pl.BlockSpec((spt, CONV_W - 1, WIDTH), lambda i: (stile(i), 0, 0)),
                   pl.BlockSpec((spt, WIDTH), lambda i: (stile(i), 0))],
        out_shape=[jax.ShapeDtypeStruct((t, WIDTH), BF16),
                   jax.ShapeDtypeStruct((n_pseq, CONV_W - 1, WIDTH), F32),
                   jax.ShapeDtypeStruct((n_pseq, 1, WIDTH), F32),
                   jax.ShapeDtypeStruct((n_sseq, CONV_W - 1, WIDTH), F32),
                   jax.ShapeDtypeStruct((n_sseq, WIDTH), F32)],
        scratch_shapes=[pltpu.VMEM((tt + SUBLANES, WIDTH), F32), pltpu.VMEM((tt, WIDTH), F32),
                        pltpu.VMEM((tt, WIDTH), F32), pltpu.VMEM((1, WIDTH), F32)],
        compiler_params=_cparams(("arbitrary",)),
        name="rglru",
    )(rest, rest, conv0, h0, cw, cb, wa, ba, wx, bx, lam)


def _ssd_chunk(xc_ref, z_ref, dtr_ref, y_ref, hst_ref, r0, q, dtb_ref, alog_ref, dsk_ref, gn_ref, exp_ref,
               tril_ref, eye_ref):
    xs = xc_ref[r0:r0 + q, 0:WIDTH]
    bm = xc_ref[r0:r0 + q, WIDTH:WIDTH + 2 * SSM_STATE].astype(BF16)
    cm = xc_ref[r0:r0 + q, WIDTH + 2 * SSM_STATE:SSM_CONV_DIM].astype(BF16)
    dt = jax.nn.softplus(dtr_ref[r0:r0 + q, :] + dtb_ref[...])
    da = dt * (-jnp.exp(alog_ref[...]))
    a_cum = _sel_left(tril_ref[0:q, 0:q], da)
    a_exp = _sel_right(a_cum, exp_ref[...])
    dt_exp = _sel_right(dt, exp_ref[...])
    a_cum_t = sum(_dot_nt(eye_ref[...], p) for p in _split3(a_cum))
    a_last = a_cum[q - 1:q, :]
    xdt = xs * dt_exp
    xdtb = xdt.astype(BF16)
    xw = (xdt * jnp.exp(a_exp[q - 1:q, :] - a_exp)).astype(BF16)
    ea = jnp.exp(a_exp)
    causal = _iota((q, q), 0) >= _iota((q, q), 1)
    for g in range(2):
        bg = bm[:, SSM_STATE * g:SSM_STATE * (g + 1)]
        cg = cm[:, SSM_STATE * g:SSM_STATE * (g + 1)]
        cb = _dot_nt(cg, bg)
        for e in range(4 * g, 4 * g + 4):
            hs = slice(HEAD_DIM * e, HEAD_DIM * (e + 1))
            seg = a_cum[:, e:e + 1] - a_cum_t[e:e + 1, :]
            m = (cb * jnp.exp(jnp.where(causal, seg, -1e30))).astype(BF16)
            h_old = hst_ref[e]
            y = _dot(m, xdtb[:, hs]) + _dot_nt(cg, h_old.astype(BF16)) * ea[:, hs]
            y_ref[r0:r0 + q, hs] = y
            decay = jnp.exp(jnp.broadcast_to(a_last[:, e:e + 1], (1, SSM_STATE)))
            hst_ref[e] = decay * h_old + _dot_tn(xw[:, hs], bg)
    y = y_ref[r0:r0 + q, :] + dsk_ref[...] * xs
    y = y * _silu(z_ref[r0:r0 + q, :])
    half = WIDTH // 2
    outs = []
    for g in range(2):
        yg = y[:, half * g:half * (g + 1)]
        outs.append(yg * lax.rsqrt(jnp.mean(yg * yg, axis=-1, keepdims=True) + EPS))
    return jnp.concatenate(outs, axis=1) * gn_ref[...]


def _ssd_body(n_ptiles, tps, seg, q_prompt, xbc_ref, z_ref, dtr_ref, conv0_ref, h0_ref, cw_ref, cb_ref, dtb_ref,
              alog_ref, dsk_ref, gn_ref, exp_ref, tril_ref, eye_ref, o_ref, convp_ref, hp_ref, convs_ref, hs_ref,
              xp_ref, xc_ref, y_ref, hst_ref):
    i = pl.program_id(0)
    tt = xbc_ref.shape[0]
    params = (dtb_ref, alog_ref, dsk_ref, gn_ref, exp_ref, tril_ref, eye_ref)
    tail = slice(SUBLANES - (CONV_W - 1), SUBLANES)

    @pl.when(i < n_ptiles)
    def _():
        @pl.when(i % tps == 0)
        def _():
            xp_ref[0:SUBLANES, :] = jnp.zeros((SUBLANES, SSM_CONV_DIM), F32)
            hst_ref[...] = jnp.zeros_like(hst_ref)

        xp_ref[SUBLANES:SUBLANES + tt, :] = xbc_ref[...]
        xc_ref[...] = _silu(_causal_conv(xp_ref, tt, cw_ref, cb_ref))
        for c in range(tt // q_prompt):
            r0 = c * q_prompt
            o_ref[r0:r0 + q_prompt, :] = _ssd_chunk(xc_ref, z_ref, dtr_ref, y_ref, hst_ref, r0, q_prompt,
                                                    *params).astype(BF16)
        xp_ref[0:SUBLANES, :] = xp_ref[tt:tt + SUBLANES, :]

        @pl.when(i % tps == tps - 1)
        def _():
            convp_ref[0] = xp_ref[tail, :]
            hp_ref[0] = hst_ref[...]

    @pl.when(i >= n_ptiles)
    def _():
        for s in range(tt // seg):
            r0 = s * seg
            xp_ref[tail, :] = conv0_ref[s]
            xp_ref[SUBLANES:SUBLANES + seg, :] = xbc_ref[r0:r0 + seg, :]
            xc_ref[r0:r0 + seg, :] = _silu(_causal_conv(xp_ref, seg, cw_ref, cb_ref))
            hst_ref[...] = h0_ref[s]
            o_ref[r0:r0 + seg, :] = _ssd_chunk(xc_ref, z_ref, dtr_ref, y_ref, hst_ref, r0, seg,
                                               *params).astype(BF16)
            convs_ref[s] = xp_ref[seg + SUBLANES - (CONV_W - 1):seg + SUBLANES, :]
            hs_ref[s] = hst_ref[...]


def _ssd(rest, conv0, h0, cw, cb, dtb, alog, dsk, gn, expand, tril, eye, n_pseq, seq_len, n_sseq, seg):
    t = rest.shape[0]
    tt = TOKEN_TILE
    tps = seq_len // tt
    n_ptiles = n_pseq * tps
    spt = tt // seg
    pseq, stile = _mixer_specs(n_ptiles, tps, spt)
    fixed = lambda i: (0, 0)
    hshape = (N_HEADS, HEAD_DIM, SSM_STATE)
    return pl.pallas_call(
        functools.partial(_ssd_body, n_ptiles, tps, seg, SSD_CHUNK),
        grid=(t // tt,),
        in_specs=[pl.BlockSpec((tt, SSM_CONV_DIM), lambda i: (i, REST_XBC // SSM_CONV_DIM)),
                  pl.BlockSpec((tt, WIDTH), lambda i: (i, REST_ZC // WIDTH)),
                  pl.BlockSpec((tt, LANES), lambda i: (i, REST_DT // LANES)),
                  pl.BlockSpec((spt, CONV_W - 1, SSM_CONV_DIM), lambda i: (stile(i), 0, 0)),
                  pl.BlockSpec((spt,) + hshape, lambda i: (stile(i), 0, 0, 0)),
                  pl.BlockSpec((CONV_W, SSM_CONV_DIM), fixed), pl.BlockSpec((1, SSM_CONV_DIM), fixed),
                  pl.BlockSpec((1, LANES), fixed), pl.BlockSpec((1, LANES), fixed),
                  pl.BlockSpec((1, WIDTH), fixed), pl.BlockSpec((1, WIDTH), fixed),
                  pl.BlockSpec((LANES, WIDTH), fixed), pl.BlockSpec((KEY_BLOCK, KEY_BLOCK), fixed),
                  pl.BlockSpec((SUBLANES, LANES), fixed)],
        out_specs=[pl.BlockSpec((tt, WIDTH), lambda i: (i, 0)),
                   pl.BlockSpec((1, CONV_W - 1, SSM_CONV_DIM), lambda i: (pseq(i), 0, 0)),
                   pl.BlockSpec((1,) + hshape, lambda i: (pseq(i), 0, 0, 0)),
                   pl.BlockSpec((spt, CONV_W - 1, SSM_CONV_DIM), lambda i: (stile(i), 0, 0)),
                   pl.BlockSpec((spt,) + hshape, lambda i: (stile(i), 0, 0, 0))],
        out_shape=[jax.ShapeDtypeStruct((t, WIDTH), BF16),
                   jax.ShapeDtypeStruct((n_pseq, CONV_W - 1, SSM_CONV_DIM), F32),
                   jax.ShapeDtypeStruct((n_pseq,) + hshape, F32),
                   jax.ShapeDtypeStruct((n_sseq, CONV_W - 1, SSM_CONV_DIM), F32),
                   jax.ShapeDtypeStruct((n_sseq,) + hshape, F32)],
        scratch_shapes=[pltpu.VMEM((tt + SUBLANES, SSM_CONV_DIM), F32), pltpu.VMEM((tt, SSM_CONV_DIM), F32),
                        pltpu.VMEM((tt, WIDTH), F32), pltpu.VMEM(hshape, F32)],
        compiler_params=_cparams(("arbitrary",)),
        name="ssd",
    )(rest, rest, rest, conv0, h0, cw, cb, dtb, alog, dsk, gn, expand, tril, eye)


def _route(lt, be_ref, bg_ref):
    n = lt.shape[1]
    le = lt[0:N_EXPERTS, :] + be_ref[:, 0:1]
    lg = lt[N_EXPERTS:N_EXPERTS + SUBLANES, :] + bg_ref[:, 0:1]
    gmax = jnp.max(lg, axis=0, keepdims=True)
    gi = _iota((SUBLANES, n), 0)
    g_sel = jnp.min(jnp.where(lg == gmax, gi, SUBLANES), axis=0, keepdims=True)
    p_sel = 1.0 / jnp.sum(jnp.exp(lg - gmax), axis=0, keepdims=True)
    ei = _iota((N_EXPERTS, n), 0)
    m1 = jnp.where((ei >> 3) == g_sel, le, -jnp.inf)
    v1 = jnp.max(m1, axis=0, keepdims=True)
    i1 = jnp.min(jnp.where(m1 == v1, ei, N_EXPERTS), axis=0, keepdims=True)
    m2 = jnp.where(ei == i1, -jnp.inf, m1)
    v2 = jnp.max(m2, axis=0, keepdims=True)
    i2 = jnp.min(jnp.where(m2 == v2, ei, N_EXPERTS), axis=0, keepdims=True)
    e2 = jnp.exp(v2 - v1)
    w1 = p_sel / (1.0 + e2)
    w2 = w1 * e2
    r = _iota((SUBLANES, n), 0)
    out = jnp.where(r == 0, i1.astype(F32), 0.0)
    out = jnp.where(r == 1, i2.astype(F32), out)
    out = jnp.where(r == 2, w1, out)
    return jnp.where(r == 3, w2, out)


def _merge_body(n_ptiles, h_ref, oap_ref, oas_ref, ob_ref, oc_ref, gmix_ref, wg_ref, bg_ref, wbo_ref, wout_ref,
                gffn_ref, wr_ref, bre_ref, brg_ref, h1_ref, u2_ref, route_ref, ya_ref):
    i = pl.program_id(0)
    x = h_ref[...]
    ub = _rmsnorm(x, gmix_ref[...]).astype(BF16)

    @pl.when(i < n_ptiles)
    def _():
        ya_ref[...] = _dot(oap_ref[...], wbo_ref[0:WIDTH, :])

    @pl.when(i >= n_ptiles)
    def _():
        ya_ref[...] = _dot(oas_ref[...].astype(BF16), wbo_ref[0:WIDTH, :])

    def gate(b):
        return jax.nn.sigmoid(_dot(ub, wg_ref[:, D_MODEL * b:D_MODEL * (b + 1)])
                              + bg_ref[:, D_MODEL * b:D_MODEL * (b + 1)])

    merged = gate(0) * ya_ref[...]
    merged = merged + gate(1) * _dot(ob_ref[...], wbo_ref[WIDTH:2 * WIDTH, :])
    merged = merged + gate(2) * _dot(oc_ref[...], wbo_ref[2 * WIDTH:3 * WIDTH, :])
    h1 = x + _dot(merged.astype(BF16), wout_ref[...])
    h1_ref[...] = h1
    u2 = _rmsnorm(h1, gffn_ref[...])
    u2_ref[...] = u2
    lt = _dot_nt(wr_ref[...].astype(BF16), u2.astype(BF16))
    route_ref[...] = _route(lt, bre_ref, brg_ref)


def _merge(h, oa_p, oa_s, ob, oc, gmix, wg, bg, wbo, wout, gffn, wr, bre, brg):
    t = h.shape[0]
    tm = PROJ_TILE
    n_ptiles = oa_p.shape[0] // tm
    row = lambda i: (i, 0)
    fixed = lambda i: (0, 0)
    full = lambda a: pl.BlockSpec(a.shape, fixed)
    return pl.pallas_call(
        functools.partial(_merge_body, n_ptiles),
        grid=(t // tm,),
        in_specs=[pl.BlockSpec((tm, D_MODEL), row),
                  pl.BlockSpec((tm, WIDTH), lambda i: (jnp.minimum(i, n_ptiles - 1), 0)),
                  pl.BlockSpec((tm, WIDTH), lambda i: (jnp.maximum(i - n_ptiles, 0), 0)),
                  pl.BlockSpec((tm, WIDTH), row), pl.BlockSpec((tm, WIDTH), row),
                  full(gmix), full(wg), full(bg), full(wbo), full(wout), full(gffn), full(wr), full(bre), full(brg)],
        out_specs=[pl.BlockSpec((tm, D_MODEL), row), pl.BlockSpec((tm, D_MODEL), row),
                   pl.BlockSpec((SUBLANES, tm), lambda i: (0, i))],
        out_shape=[jax.ShapeDtypeStruct((t, D_MODEL), F32), jax.ShapeDtypeStruct((t, D_MODEL), F32),
                   jax.ShapeDtypeStruct((SUBLANES, t), F32)],
        scratch_shapes=[pltpu.VMEM((tm, D_MODEL), F32)],
        compiler_params=_cparams(("arbitrary",)),
        name="merge",
    )(h, oa_p, oa_s, ob, oc, gmix, wg, bg, wbo, wout, gffn, wr, bre, brg)


def _rank_body(route_ref, before_ref, ltri_ref, dest_ref, meta_ref, cnt_ref, start_ref):
    p = pl.program_id(0)
    i = pl.program_id(1)
    tm = route_ref.shape[1]
    ei = _iota((N_EXPERTS, tm), 0)
    hot0 = ei == route_ref[0:1, :].astype(I32)
    hot1 = ei == route_ref[1:2, :].astype(I32)
    both = jnp.where(jnp.logical_or(hot0, hot1), 1.0, 0.0)
    tile_cnt = jnp.broadcast_to(jnp.sum(both, axis=1, keepdims=True), (N_EXPERTS, LANES))

    @pl.when(jnp.logical_and(p == 0, i == 0))
    def _():
        cnt_ref[...] = jnp.zeros_like(cnt_ref)

    @pl.when(jnp.logical_and(p == 1, i == 0))
    def _():
        cnt = cnt_ref[...]
        padded = jnp.floor((cnt + (EXPERT_BLOCK - 1)) * (1.0 / EXPERT_BLOCK)) * EXPERT_BLOCK
        start = _sel_left(ltri_ref[...], padded)
        start_ref[...] = start
        lane = _iota((N_EXPERTS, LANES), 1)
        meta_ref[...] = jnp.where(lane == 0, cnt, jnp.where(lane == 1, start, 0.0))
        cnt_ref[...] = jnp.zeros_like(cnt_ref)

    @pl.when(p == 1)
    def _():
        prior = _dot(both.astype(BF16), before_ref[...])
        slot = prior + (start_ref[:, 0:1] + cnt_ref[:, 0:1])
        d0 = jnp.sum(jnp.where(hot0, slot, 0.0), axis=0, keepdims=True)
        d1 = jnp.sum(jnp.where(hot1, slot, 0.0), axis=0, keepdims=True)
        dest_ref[0] = jnp.concatenate([d0, d1], axis=0).astype(I32)

    cnt_ref[...] = cnt_ref[...] + tile_cnt


def _rank(route, before, ltri):
    t = route.shape[1]
    tm = TOKEN_TILE
    nt = t // tm
    return pl.pallas_call(
        _rank_body,
        grid=(2, nt),
        in_specs=[pl.BlockSpec((SUBLANES, tm), lambda p, i: (0, i)),
                  pl.BlockSpec((tm, tm), lambda p, i: (0, 0)),
                  pl.BlockSpec((N_EXPERTS, N_EXPERTS), lambda p, i: (0, 0))],
        out_specs=[pl.BlockSpec((1, 2, tm), lambda p, i: (i * p, 0, 0)),
                   pl.BlockSpec((N_EXPERTS, LANES), lambda p, i: (0, 0))],
        out_shape=[jax.ShapeDtypeStruct((nt, 2, tm), I32), jax.ShapeDtypeStruct((N_EXPERTS, LANES), F32)],
        scratch_shapes=[pltpu.VMEM((N_EXPERTS, LANES), F32), pltpu.VMEM((N_EXPERTS, LANES), F32)],
        compiler_params=_cparams(("arbitrary", "arbitrary")),
        name="moe_rank",
    )(route, before, ltri)


def _row_copy(src_ref, src_row, dst_ref, dst_row, sem):
    return pltpu.make_async_copy(src_ref.at[pl.ds(src_row, 1), :], dst_ref.at[pl.ds(dst_row, 1), :], sem)


def _dispatch_body(dest_ref, u_ref, xs_in_ref, xs_ref, sem):
    del xs_in_ref
    tm = u_ref.shape[0]

    def start(t, c):
        for k in range(2):
            _row_copy(u_ref, t, xs_ref, dest_ref[0, k, t], sem).start()
        return c

    def wait(t, c):
        for k in range(2):
            _row_copy(u_ref, t, xs_ref, dest_ref[0, k, t], sem).wait()
        return c

    lax.fori_loop(0, tm, start, 0, unroll=ROW_DMA_UNROLL)
    lax.fori_loop(0, tm, wait, 0, unroll=ROW_DMA_UNROLL)


def _dispatch(dest, u2, xs_zero):
    t = u2.shape[0]
    tm = dest.shape[2]
    return pl.pallas_call(
        _dispatch_body,
        grid=(t // tm,),
        in_specs=[pl.BlockSpec((1, 2, tm), lambda i: (i, 0, 0), memory_space=pltpu.SMEM),
                  pl.BlockSpec((tm, D_MODEL), lambda i: (i, 0)),
                  pl.BlockSpec(memory_space=pl.ANY)],
        out_specs=pl.BlockSpec(memory_space=pl.ANY),
        out_shape=jax.ShapeDtypeStruct(xs_zero.shape, F32),
        scratch_shapes=[pltpu.SemaphoreType.DMA(())],
        input_output_aliases={2: 0},
        compiler_params=_cparams(("arbitrary",), disable_bounds_checks=True),
        name="moe_dispatch",
    )(dest, u2, xs_zero)


def _experts_body(be_ref, nu_ref, x_ref, wg_ref, wu_ref, wd_ref, y_ref, wgb_ref, wub_ref, wdb_ref):
    i = pl.program_id(0)
    live = i < nu_ref[0]

    @pl.when(jnp.logical_or(i == 0, be_ref[i] != be_ref[jnp.maximum(i - 1, 0)]))
    def _():
        wgb_ref[...] = wg_ref[0, 0].astype(BF16)
        wub_ref[...] = wu_ref[0, 0].astype(BF16)
        wdb_ref[...] = wd_ref[0, 0].astype(BF16)

    @pl.when(live)
    def _():
        xb = x_ref[...].astype(BF16)
        a = _silu(_dot(xb, wgb_ref[...])) * _dot(xb, wub_ref[...])
        y_ref[...] = _dot(a.astype(BF16), wdb_ref[...])

    @pl.when(jnp.logical_not(live))
    def _():
        y_ref[...] = jnp.zeros_like(y_ref)


def _experts(blk_expert, n_used, xs, wg, wu, wd, layer):
    cap = xs.shape[0]
    nb = cap // EXPERT_BLOCK
    live = lambda i, nu: jnp.minimum(i, nu[0] - 1)
    rows = lambda i, be, nu: (live(i, nu), 0)
    wsel = lambda i, be, nu: (layer, be[live(i, nu)], 0, 0)
    return pl.pallas_call(
        _experts_body,
        grid_spec=pltpu.PrefetchScalarGridSpec(
            num_scalar_prefetch=2,
            grid=(nb,),
            in_specs=[pl.BlockSpec((EXPERT_BLOCK, D_MODEL), rows),
                      pl.BlockSpec((1, 1, D_MODEL, D_EXPERT), wsel), pl.BlockSpec((1, 1, D_MODEL, D_EXPERT), wsel),
                      pl.BlockSpec((1, 1, D_EXPERT, D_MODEL), wsel)],
            out_specs=pl.BlockSpec((EXPERT_BLOCK, D_MODEL), lambda i, be, nu: (i, 0)),
            scratch_shapes=[pltpu.VMEM((D_MODEL, D_EXPERT), BF16), pltpu.VMEM((D_MODEL, D_EXPERT), BF16),
                            pltpu.VMEM((D_EXPERT, D_MODEL), BF16)]),
        out_shape=jax.ShapeDtypeStruct((cap, D_MODEL), F32),
        compiler_params=_cparams(("arbitrary",)),
        name="moe_experts",
    )(blk_expert, n_used, xs, wg, wu, wd)


def _combine_body(n_ptiles, per, dest_ref, h_ref, route_ref, g_ref, y_hbm, *rest):
    *o_refs, ybuf_ref, sem = rest
    i = pl.program_id(0)
    tm = h_ref.shape[0]
    base = (i % per) * tm

    def start(t, c):
        for k in range(2):
            _row_copy(y_hbm, dest_ref[0, k, base + t], ybuf_ref.at[k], t, sem).start()
        return c

    def wait(t, c):
        for k in range(2):
            _row_copy(y_hbm, dest_ref[0, k, base + t], ybuf_ref.at[k], t, sem).wait()
        return c

    lax.fori_loop(0, tm, start, 0, unroll=ROW_DMA_UNROLL)
    eye = _iota((tm, tm), 0) == _iota((tm, tm), 1)
    w0 = jnp.sum(jnp.where(eye, route_ref[2:3, :], 0.0), axis=1, keepdims=True)
    w1 = jnp.sum(jnp.where(eye, route_ref[3:4, :], 0.0), axis=1, keepdims=True)
    lax.fori_loop(0, tm, wait, 0, unroll=ROW_DMA_UNROLL)
    h2 = h_ref[...] + (w0 * ybuf_ref[0] + w1 * ybuf_ref[1])
    if len(o_refs) == 1:
        o_refs[0][...] = h2
    else:
        @pl.when(i < n_ptiles)
        def _():
            o_refs[0][...] = _rmsnorm(h2, g_ref[...])

        @pl.when(i >= n_ptiles)
        def _():
            o_refs[1][...] = _rmsnorm(h2, g_ref[...])


def _combine(dest, h1, route, g_final, y, t_p, final):
    t = h1.shape[0]
    tm = PROJ_TILE
    per = dest.shape[2] // tm
    n_ptiles = t_p // tm
    if final:
        out_specs = [pl.BlockSpec((tm, D_MODEL), lambda i: (jnp.minimum(i, n_ptiles - 1), 0)),
                     pl.BlockSpec((tm, D_MODEL), lambda i: (jnp.maximum(i - n_ptiles, 0), 0))]
        out_shape = [jax.ShapeDtypeStruct((t_p, D_MODEL), F32), jax.ShapeDtypeStruct((t - t_p, D_MODEL), F32)]
    else:
        out_specs = pl.BlockSpec((tm, D_MODEL), lambda i: (i, 0))
        out_shape = jax.ShapeDtypeStruct((t, D_MODEL), F32)
    return pl.pallas_call(
        functools.partial(_combine_body, n_ptiles, per),
        grid=(t // tm,),
        in_specs=[pl.BlockSpec((1, 2, dest.shape[2]), lambda i: (i // per, 0, 0), memory_space=pltpu.SMEM),
                  pl.BlockSpec((tm, D_MODEL), lambda i: (i, 0)),
                  pl.BlockSpec((SUBLANES, tm), lambda i: (0, i)),
                  pl.BlockSpec((1, D_MODEL), lambda i: (0, 0)),
                  pl.BlockSpec(memory_space=pl.ANY)],
        out_specs=out_specs,
        out_shape=out_shape,
        scratch_shapes=[pltpu.VMEM((2, tm, D_MODEL), F32), pltpu.SemaphoreType.DMA(())],
        compiler_params=_cparams(("arbitrary",), disable_bounds_checks=True),
        name="moe_combine",
    )(dest, h1, route, g_final, y)


def _constants():
    r = jnp.arange(3 * KEY_BLOCK)
    c = jnp.arange(2 * KEY_BLOCK)
    tri = jnp.where(c[None, :] < KEY_BLOCK, (r[:, None] % KEY_BLOCK) > c[None, :], True).astype(BF16)
    q = jnp.arange(KEY_BLOCK)
    tril = (q[:, None] >= q[None, :]).astype(BF16)
    lane = jnp.arange(WIDTH)
    expand = (jnp.arange(LANES)[:, None] == lane[None, :] // HEAD_DIM).astype(BF16)
    eye = (jnp.arange(SUBLANES)[:, None] == jnp.arange(LANES)[None, :]).astype(BF16)
    t = jnp.arange(TOKEN_TILE)
    before = (t[:, None] < t[None, :]).astype(BF16)
    e = jnp.arange(N_EXPERTS)
    ltri = (e[:, None] > e[None, :]).astype(BF16)
    return tri, tril, expand, eye, before, ltri


def _block_diag(w):
    n, d = w.shape[0], w.shape[1]
    out = jnp.zeros((n, d, n, d), w.dtype)
    out = out.at[jnp.arange(n), :, jnp.arange(n), :].set(w)
    return out.reshape(n * d, n * d)


def _proj_weight(w_in):
    wq = w_in[:, 0:WIDTH].reshape(D_MODEL, N_HEADS, 1, HEAD_DIM)
    half = (jnp.arange(N_HEADS) % 2)[None, :, None, None] == jnp.arange(2)[None, None, :, None]
    wqm = jnp.where(half, wq, 0.0).reshape(D_MODEL, QM_W)
    o = 3 * WIDTH
    xb, gb, zc = w_in[:, o:o + WIDTH], w_in[:, o + WIDTH:o + 2 * WIDTH], w_in[:, o + 2 * WIDTH:o + 3 * WIDTH]
    xbc = w_in[:, o + 3 * WIDTH:o + 3 * WIDTH + SSM_CONV_DIM]
    dt = jnp.pad(w_in[:, o + 3 * WIDTH + SSM_CONV_DIM:], ((0, 0), (0, LANES - N_HEADS)))
    return jnp.concatenate([wqm, w_in[:, WIDTH:3 * WIDTH], xbc, xb, gb, zc, dt], axis=1).astype(BF16)


def _pad_lanes(v):
    return jnp.pad(v, (0, LANES - v.shape[0]))[None, :]


def kernel(x_prompt, x_sample, cache_sb_k, cache_sb_v, state_lru_conv, state_lru_h, state_ssm_conv, state_ssm_h, g_mix, w_in, w_gate, b_gate, w_branch_out, w_out, lru_conv_w, lru_conv_b, lru_w_a, lru_b_a, lru_w_x, lru_b_x, lru_lambda, ssm_conv_w, ssm_conv_b, ssm_dt_bias, ssm_a_log, ssm_d, ssm_norm_g, g_ffn, w_router_group, b_router_group, w_router_expert, b_router_expert, w_expert_gate, w_expert_up, w_expert_down, g_final):
    n_p, s_p, _ = x_prompt.shape
    n_s, s_s, _ = x_sample.shape
    depth = w_in.shape[0]
    past = cache_sb_k.shape[2]
    t_p, t_s = n_p * s_p, n_s * s_s
    t = t_p + t_s
    assert s_p % TOKEN_TILE == 0 and TOKEN_TILE % s_s == 0 and t_s % TOKEN_TILE == 0
    assert s_s % SUBLANES == 0 and past % SAMPLE_WINDOW == 0

    tri, tril, expand, eye, before, ltri = _constants()
    cache_k = jnp.transpose(cache_sb_k, (0, 1, 3, 4, 2))
    cache_v = jnp.transpose(cache_sb_v, (0, 1, 3, 4, 2))
    n_blocks = -(-2 * t // EXPERT_BLOCK) + N_EXPERTS
    xs_zero = jnp.zeros((n_blocks * EXPERT_BLOCK, D_MODEL), F32)

    h = jnp.concatenate([x_prompt.reshape(t_p, D_MODEL), x_sample.reshape(t_s, D_MODEL)], axis=0)
    states = []
    for l in range(depth):
        row = lambda a: a[l][None, :]
        qm, k_p, v_p, k_s, v_s, kb, vb, rest = _inproj(h, row(g_mix), _proj_weight(w_in[l]), n_p, s_p)

        oa_p = _attn_prompt(qm, kb, vb, tri, n_p, s_p)
        acc, rp = _attn_sample(qm, kb, vb, cache_k, cache_v, tri, l, t_p, n_s, s_s)
        if past > SAMPLE_WINDOW:
            older = functools.partial(_attn_older, qm, cache_k, cache_v, tri=tri, layer=l, row0=t_p, n_seq=n_s,
                                      tq=s_s)
            oa_s = lax.cond(jnp.max(rp[:, :N_HEADS]) > LOG_CUTOFF,
                            lambda a, r: older(acc=a, rp=r), lambda a, r: a, acc, rp)
        else:
            oa_s = acc

        ob, lconv_p, lh_p, lconv_s, lh_s = _lru(
            rest, state_lru_conv[l], state_lru_h[l], lru_conv_w[l], row(lru_conv_b),
            _block_diag(lru_w_a[l]).astype(BF16), row(lru_b_a), _block_diag(lru_w_x[l]).astype(BF16),
            row(lru_b_x), row(lru_lambda), n_p, s_p, n_s, s_s)

        oc, sconv_p, sh_p, sconv_s, sh_s = _ssd(
            rest, state_ssm_conv[l], state_ssm_h[l], ssm_conv_w[l], row(ssm_conv_b),
            _pad_lanes(ssm_dt_bias[l]), _pad_lanes(ssm_a_log[l]), jnp.repeat(ssm_d[l], HEAD_DIM)[None, :],
            row(ssm_norm_g), expand, tril, eye, n_p, s_p, n_s, s_s)

        w_r = jnp.concatenate([w_router_expert[l].T, w_router_group[l].T,
                               jnp.zeros((SUBLANES - N_GROUPS, D_MODEL), F32)], axis=0)
        b_re = jnp.broadcast_to(b_router_expert[l][:, None], (N_EXPERTS, LANES))
        b_rg = jnp.broadcast_to(jnp.concatenate([b_router_group[l], jnp.full((SUBLANES - N_GROUPS,), -1e30, F32)])[:, None],
                                (SUBLANES, LANES))
        h1, u2, route = _merge(h, oa_p, oa_s, ob, oc, row(g_mix), w_gate[l].astype(BF16), row(b_gate),
                               w_branch_out[l].astype(BF16), w_out[l].astype(BF16), row(g_ffn), w_r, b_re, b_rg)

        dest, meta = _rank(route, before, ltri)
        counts, starts = meta[:, 0], meta[:, 1]
        ends = starts + jnp.ceil(counts / EXPERT_BLOCK) * EXPERT_BLOCK
        blk_row = (jnp.arange(n_blocks) * EXPERT_BLOCK).astype(F32)
        blk_expert = jnp.minimum(jnp.sum(ends[None, :] <= blk_row[:, None], axis=1), N_EXPERTS - 1).astype(I32)
        n_used = (ends[N_EXPERTS - 1:] / EXPERT_BLOCK).astype(I32)

        xs = _dispatch(dest, u2, xs_zero)
        y = _experts(blk_expert, n_used, xs, w_expert_gate, w_expert_up, w_expert_down, l)
        h = _combine(dest, h1, route, g_final[None, :], y, t_p, final=(l == depth - 1))

        head = lambda a, n, s: a.reshape(n, s, N_HEADS, HEAD_DIM)
        states.append((k_p, v_p, lconv_p, lh_p[:, 0], sconv_p, sh_p,
                       head(k_s, n_s, s_s), head(v_s, n_s, s_s), lconv_s, lh_s, sconv_s, sh_s))

    stacked = [jnp.stack([st[j] for st in states], axis=0) for j in range(12)]
    for j in range(2):
        kt = stacked[j].reshape(depth, n_p, N_HEADS, HEAD_DIM, s_p)
        stacked[j] = jnp.transpose(kt, (0, 1, 4, 2, 3))
    y_p, y_s = h
    return (y_p.reshape(n_p, s_p, D_MODEL), y_s.reshape(n_s, s_s, D_MODEL)) + tuple(stacked)
```

```python
import functools

import jax
import jax.numpy as jnp
from jax import lax
from jax.experimental import pallas as pl
from jax.experimental.pallas import tpu as pltpu

F32 = jnp.float32
BF16 = jnp.bfloat16
I32 = jnp.int32

D_MODEL = 1024
N_HEADS = 8
HEAD_DIM = 64
WIDTH = 512
CONV_W = 4
SSM_STATE = 128
SSM_CONV_DIM = 1024
N_GROUPS = 4
PER_GROUP = 8
N_EXPERTS = 32
D_EXPERT = 512
LRU_C = 8.0
EPS = 1e-6

LANES = 128
SUBLANES = 8
TOKEN_TILE = 512
PROJ_TILE = 512
KEY_BLOCK = 128
SSD_CHUNK = 64
SAMPLE_WINDOW = 512
EXPERT_BLOCK = 512
VMEM_LIMIT = 56 * 1024 * 1024

LOG_CUTOFF = -88.0

REST_XBC, REST_XB, REST_GB, REST_ZC, REST_DT = 0, 1024, 1536, 2048, 2560
REST_W = 2688
QM_W = 1024
PROJ_W = QM_W + 2 * WIDTH + REST_W


def _cparams(sem, **kw):
    return pltpu.CompilerParams(dimension_semantics=sem, vmem_limit_bytes=VMEM_LIMIT, **kw)


ROW_DMA_UNROLL = 8


def _split3(x):
    hi = x.astype(BF16)
    r = x - hi.astype(F32)
    mid = r.astype(BF16)
    lo = (r - mid.astype(F32)).astype(BF16)
    return hi, mid, lo


def _dot(a, b):
    return jnp.dot(a, b, preferred_element_type=F32)


def _dot_nt(a, b):
    return lax.dot_general(a, b, (((1,), (1,)), ((), ())), preferred_element_type=F32)


def _dot_tn(a, b):
    return lax.dot_general(a, b, (((0,), (0,)), ((), ())), preferred_element_type=F32)


def _sel_right(x, m01):
    return sum(_dot(p, m01) for p in _split3(x))


def _sel_left(m01, x):
    return sum(_dot(m01, p) for p in _split3(x))


def _rmsnorm(x, g):
    return x * lax.rsqrt(jnp.mean(x * x, axis=-1, keepdims=True) + EPS) * g


def _log_sigmoid(z):
    return jnp.minimum(z, 0.0) - jnp.log1p(jnp.exp(-jnp.abs(z)))


def _silu(x):
    return x * jax.nn.sigmoid(x)


def _iota(shape, dim):
    return lax.broadcasted_iota(I32, shape, dim)


def _store_heads(ref, x):
    rows = x.shape[0]
    for h in range(N_HEADS):
        ref[pl.ds(h, rows, stride=N_HEADS), :] = x[:, HEAD_DIM * h:HEAD_DIM * (h + 1)]


def _inproj_body(n_ptiles, h_ref, g_ref, w_ref, qm_ref, kp_ref, vp_ref, ks_ref, vs_ref, kb_ref, vb_ref, rest_ref):
    i = pl.program_id(0)
    ub = _rmsnorm(h_ref[...], g_ref[...]).astype(BF16)

    def proj(c0, c1):
        return _dot(ub, w_ref[:, c0:c1])

    for c in range(0, QM_W, 512):
        qm_ref[:, c:c + 512] = proj(c, c + 512).astype(BF16)
    k = proj(QM_W, QM_W + WIDTH)
    kb_ref[...] = k.astype(BF16)
    v = proj(QM_W + WIDTH, QM_W + 2 * WIDTH)
    vb_ref[...] = v.astype(BF16)

    @pl.when(i < n_ptiles)
    def _():
        kp_ref[0] = k.T
        vp_ref[0] = v.T

    @pl.when(i >= n_ptiles)
    def _():
        _store_heads(ks_ref, k)
        _store_heads(vs_ref, v)

    base = QM_W + 2 * WIDTH
    for c in range(0, REST_W, 512):
        c1 = min(c + 512, REST_W)
        rest_ref[:, c:c1] = proj(base + c, base + c1)


def _inproj(h, g, w, n_p, s_p):
    t = h.shape[0]
    tm = PROJ_TILE
    t_p = n_p * s_p
    n_ptiles = t_p // tm
    row = lambda i: (i, 0)
    fixed = lambda i: (0, 0)
    tps = s_p // tm
    ptile = lambda i: jnp.minimum(i, n_ptiles - 1)
    prow = pl.BlockSpec((1, WIDTH, tm), lambda i: (ptile(i) // tps, 0, ptile(i) % tps))
    pshape = jax.ShapeDtypeStruct((n_p, WIDTH, s_p), F32)
    srow = pl.BlockSpec((tm * N_HEADS, HEAD_DIM), lambda i: (jnp.maximum(i - n_ptiles, 0), 0))
    heads = lambda n: jax.ShapeDtypeStruct((n * N_HEADS, HEAD_DIM), F32)
    return pl.pallas_call(
        functools.partial(_inproj_body, n_ptiles),
        grid=(t // tm,),
        in_specs=[pl.BlockSpec((tm, D_MODEL), row), pl.BlockSpec((1, D_MODEL), fixed),
                  pl.BlockSpec((D_MODEL, PROJ_W), fixed, pipeline_mode=pl.Buffered(1))],
        out_specs=[pl.BlockSpec((tm, QM_W), row), prow, prow, srow, srow, pl.BlockSpec((tm, WIDTH), row),
                   pl.BlockSpec((tm, WIDTH), row), pl.BlockSpec((tm, REST_W), row)],
        out_shape=[jax.ShapeDtypeStruct((t, QM_W), BF16), pshape, pshape, heads(t - t_p), heads(t - t_p),
                   jax.ShapeDtypeStruct((t, WIDTH), BF16), jax.ShapeDtypeStruct((t, WIDTH), BF16),
                   jax.ShapeDtypeStruct((t, REST_W), F32)],
        compiler_params=_cparams(("arbitrary",)),
        name="inproj",
    )(h, g, w)


def _sb_weights(zs, mask, r_ref, tri_ref):
    kb = zs[0].shape[1]
    log_beta, log_keep = [], []
    for z in zs:
        ls = _log_sigmoid(z)
        lk = ls - z
        log_beta.append(ls)
        log_keep.append(lk if mask is None else jnp.where(mask, lk, 0.0))
    sums = []
    for lk in log_keep:
        if kb == KEY_BLOCK:
            res = _dot(jnp.concatenate(_split3(lk), axis=1), tri_ref[...])
            sums.append((res[:, :KEY_BLOCK], res[:, KEY_BLOCK:]))
        else:
            sums.append((_sel_right(lk, tri_ref[0:kb, 0:kb]),
                         _sel_right(lk, tri_ref[0:kb, KEY_BLOCK:2 * KEY_BLOCK])))
    ws, top = [], None
    for h, (ls, (la, tot)) in enumerate(zip(log_beta, sums)):
        r_old = r_ref[h]
        w = jnp.exp(ls + la + r_old[:, :kb])
        ws.append((w if mask is None else jnp.where(mask, w, 0.0)).astype(BF16))
        r_new = r_old + tot
        r_ref[h] = r_new
        top = r_new if top is None else jnp.maximum(top, r_new)
    return ws, jnp.max(top)


def _sb_block_pairs(qm, kblk, vblk, mask, acc_ref, r_ref, tri_ref):
    pair = lambda x, h: x[:, LANES * (h // 2):LANES * (h // 2 + 1)]
    zs = [_dot_nt(qm[:, LANES * h:LANES * (h + 1)], pair(kblk, h)) * (HEAD_DIM ** -0.5) for h in range(N_HEADS)]
    ws, rmax = _sb_weights(zs, mask, r_ref, tri_ref)
    low = _iota((qm.shape[0], LANES), 1) < HEAD_DIM
    for p in range(N_HEADS // 2):
        v2 = pair(vblk, 2 * p)
        acc_ref[:, LANES * p:LANES * (p + 1)] += jnp.where(low, _dot(ws[2 * p], v2), _dot(ws[2 * p + 1], v2))
    return rmax


def _sb_block_heads(q_heads, k_of, v_of, mask, acc_ref, r_ref, tri_ref, keys_on_lanes=False):
    qk, wv = (_dot, _dot_nt) if keys_on_lanes else (_dot_nt, _dot)
    zs = [qk(q_heads[h], k_of(h)) * (HEAD_DIM ** -0.5) for h in range(N_HEADS)]
    ws, rmax = _sb_weights(zs, mask, r_ref, tri_ref)
    for h in range(N_HEADS):
        acc_ref[:, HEAD_DIM * h:HEAD_DIM * (h + 1)] += wv(ws[h], v_of(h))
    return rmax


def _query_heads(qm):
    return [qm[:, LANES * h + HEAD_DIM * (h % 2):LANES * h + HEAD_DIM * (h % 2 + 1)] for h in range(N_HEADS)]


def _cache_heads(c_ref, block):
    return lambda h: c_ref[0, 0, h, :, KEY_BLOCK * block:KEY_BLOCK * (block + 1)].astype(BF16)


def _more_keys(c):
    j, rmax = c
    return jnp.logical_and(j >= 0, rmax > LOG_CUTOFF)


def _attn_prompt_body(qm_ref, kb_ref, vb_ref, tri_ref, o_ref, acc_ref, r_ref):
    i = pl.program_id(1)
    tq = qm_ref.shape[0]
    acc_ref[...] = jnp.zeros_like(acc_ref)
    r_ref[...] = jnp.zeros_like(r_ref)
    qm = qm_ref[...]

    def block(j, mask):
        off = pl.multiple_of(j * KEY_BLOCK, KEY_BLOCK)
        return _sb_block_pairs(qm, kb_ref[pl.ds(off, KEY_BLOCK), :], vb_ref[pl.ds(off, KEY_BLOCK), :],
                               mask, acc_ref, r_ref, tri_ref)

    rmax = block(i, _iota((tq, tq), 1) < _iota((tq, tq), 0))
    lax.while_loop(_more_keys, lambda c: (c[0] - 1, block(c[0], None)), (i - 1, rmax))
    o_ref[...] = acc_ref[...].astype(BF16)


def _attn_prompt(qm, kb, vb, tri, n_seq, seq_len):
    tq = KEY_BLOCK
    nq = seq_len // tq
    return pl.pallas_call(
        _attn_prompt_body,
        grid=(n_seq, nq),
        in_specs=[pl.BlockSpec((tq, QM_W), lambda b, i: (b * nq + i, 0)),
                  pl.BlockSpec((seq_len, WIDTH), lambda b, i: (b, 0)),
                  pl.BlockSpec((seq_len, WIDTH), lambda b, i: (b, 0)),
                  pl.BlockSpec((3 * KEY_BLOCK, 2 * KEY_BLOCK), lambda b, i: (0, 0))],
        out_specs=pl.BlockSpec((tq, WIDTH), lambda b, i: (b * nq + i, 0)),
        out_shape=jax.ShapeDtypeStruct((n_seq * seq_len, WIDTH), BF16),
        scratch_shapes=[pltpu.VMEM((tq, WIDTH), F32), pltpu.VMEM((N_HEADS, tq, LANES), F32)],
        compiler_params=_cparams(("arbitrary", "arbitrary")),
        name="attn_prompt",
    )(qm, kb, vb, tri)


def _pack_r(r_ref, tq):
    lane = _iota((tq, LANES), 1)
    rp = jnp.zeros((tq, LANES), F32)
    for h in range(N_HEADS):
        rp = jnp.where(lane == h, r_ref[h], rp)
    return rp


def _attn_sample_body(qm_ref, kn_ref, vn_ref, ck_ref, cv_ref, tri_ref, acc_out, rp_out, acc_ref, r_ref):
    tq = qm_ref.shape[0]
    acc_ref[...] = jnp.zeros_like(acc_ref)
    r_ref[...] = jnp.zeros_like(r_ref)
    q_heads = _query_heads(qm_ref[...])
    kn, vn = kn_ref[...], vn_ref[...]
    head = lambda x: (lambda h: x[:, HEAD_DIM * h:HEAD_DIM * (h + 1)])
    causal = _iota((tq, tq), 1) < _iota((tq, tq), 0)
    rmax = _sb_block_heads(q_heads, head(kn), head(vn), causal, acc_ref, r_ref, tri_ref)

    for j in reversed(range(ck_ref.shape[4] // KEY_BLOCK)):
        rmax = lax.cond(rmax > LOG_CUTOFF,
                        lambda j=j: _sb_block_heads(q_heads, _cache_heads(ck_ref, j), _cache_heads(cv_ref, j), None,
                                                    acc_ref, r_ref, tri_ref, keys_on_lanes=True),
                        lambda rmax=rmax: rmax)
    acc_out[...] = acc_ref[...]
    rp_out[...] = _pack_r(r_ref, tq)


def _attn_sample(qm, kb, vb, cache_k, cache_v, tri, layer, row0, n_seq, tq):
    past = cache_k.shape[4]
    win = min(SAMPLE_WINDOW, past)
    blk0 = row0 // tq
    cur = lambda b: (blk0 + b, 0)
    cblock = pl.BlockSpec((1, 1, N_HEADS, HEAD_DIM, win), lambda b: (layer, b, 0, 0, past // win - 1))
    return pl.pallas_call(
        _attn_sample_body,
        grid=(n_seq,),
        in_specs=[pl.BlockSpec((tq, QM_W), cur), pl.BlockSpec((tq, WIDTH), cur), pl.BlockSpec((tq, WIDTH), cur),
                  cblock, cblock,
                  pl.BlockSpec((3 * KEY_BLOCK, 2 * KEY_BLOCK), lambda b: (0, 0))],
        out_specs=[pl.BlockSpec((tq, WIDTH), lambda b: (b, 0)), pl.BlockSpec((tq, LANES), lambda b: (b, 0))],
        out_shape=[jax.ShapeDtypeStruct((n_seq * tq, WIDTH), F32), jax.ShapeDtypeStruct((n_seq * tq, LANES), F32)],
        scratch_shapes=[pltpu.VMEM((tq, WIDTH), F32), pltpu.VMEM((N_HEADS, tq, LANES), F32)],
        compiler_params=_cparams(("arbitrary",)),
        name="attn_sample",
    )(qm, kb, vb, cache_k, cache_v, tri)


def _attn_older_body(qm_ref, ck_ref, cv_ref, acc_in, rp_in, tri_ref, acc_out, acc_ref, r_ref):
    s = pl.program_id(1)
    tq = qm_ref.shape[0]

    @pl.when(s == 0)
    def _():
        acc_ref[...] = acc_in[...]
        rp = rp_in[...]
        for h in range(N_HEADS):
            r_ref[h] = jnp.broadcast_to(rp[:, h:h + 1], (tq, LANES))

    rmax = jnp.max(r_ref[0])
    for h in range(1, N_HEADS):
        rmax = jnp.maximum(rmax, jnp.max(r_ref[h]))

    @pl.when(rmax > LOG_CUTOFF)
    def _():
        _sb_block_heads(_query_heads(qm_ref[...]), _cache_heads(ck_ref, 0), _cache_heads(cv_ref, 0), None,
                        acc_ref, r_ref, tri_ref, keys_on_lanes=True)

    @pl.when(s == pl.num_programs(1) - 1)
    def _():
        acc_out[...] = acc_ref[...]


def _attn_older(qm, cache_k, cache_v, acc, rp, tri, layer, row0, n_seq, tq):
    past = cache_k.shape[4]
    win = min(SAMPLE_WINDOW, past)
    nb = (past - win) // KEY_BLOCK
    blk0 = row0 // tq
    cblock = pl.BlockSpec((1, 1, N_HEADS, HEAD_DIM, KEY_BLOCK), lambda b, s: (layer, b, 0, 0, nb - 1 - s))
    return pl.pallas_call(
        _attn_older_body,
        grid=(n_seq, nb),
        in_specs=[pl.BlockSpec((tq, QM_W), lambda b, s: (blk0 + b, 0)), cblock, cblock,
                  pl.BlockSpec((tq, WIDTH), lambda b, s: (b, 0)), pl.BlockSpec((tq, LANES), lambda b, s: (b, 0)),
                  pl.BlockSpec((3 * KEY_BLOCK, 2 * KEY_BLOCK), lambda b, s: (0, 0))],
        out_specs=pl.BlockSpec((tq, WIDTH), lambda b, s: (b, 0)),
        out_shape=jax.ShapeDtypeStruct((n_seq * tq, WIDTH), F32),
        scratch_shapes=[pltpu.VMEM((tq, WIDTH), F32), pltpu.VMEM((N_HEADS, tq, LANES), F32)],
        compiler_params=_cparams(("arbitrary", "arbitrary")),
        name="attn_older",
    )(qm, cache_k, cache_v, acc, rp, tri)


def _causal_conv(xp_ref, n, w_ref, b_ref):
    y = b_ref[...]
    for k in range(CONV_W):
        y = y + xp_ref[pl.ds(SUBLANES - (CONV_W - 1) + k, n), :] * w_ref[k:k + 1, :]
    return y


def _lru_rows(xp_ref, a_ref, b_ref, n, h0, gate, cw_ref, cb_ref, wa_ref, ba_ref, wx_ref, bx_ref, lam_ref):
    xc = _causal_conv(xp_ref, n, cw_ref, cb_ref)
    xcb = xc.astype(BF16)
    r = jax.nn.sigmoid(_dot(xcb, wa_ref[...]) + ba_ref[...])
    ig = jax.nn.sigmoid(_dot(xcb, wx_ref[...]) + bx_ref[...])
    log_a = LRU_C * r * _log_sigmoid(lam_ref[...])
    a_ref[0:n, :] = jnp.exp(log_a)
    th = jnp.tanh(log_a)
    b_ref[0:n, :] = jnp.sqrt(-2.0 * th / (1.0 - th)) * (ig * xc)
    row = _iota((SUBLANES, WIDTH), 0)

    def group(g, h):
        off = pl.multiple_of(g * SUBLANES, SUBLANES)
        a = a_ref[pl.ds(off, SUBLANES), :]
        b = b_ref[pl.ds(off, SUBLANES), :]
        for s in (1, 2, 4):
            a_prev = jnp.where(row >= s, pltpu.roll(a, s, 0), 1.0)
            b_prev = jnp.where(row >= s, pltpu.roll(b, s, 0), 0.0)
            b = b + a * b_prev
            a = a * a_prev
        hs = a * h + b
        b_ref[pl.ds(off, SUBLANES), :] = hs
        return hs[SUBLANES - 1:SUBLANES, :]

    h_last = lax.fori_loop(0, n // SUBLANES, group, h0)
    y = b_ref[0:n, :] * jax.nn.gelu(gate, approximate=True)
    return y, h_last


def _lru_body(n_ptiles, tps, seg, xb_ref, gb_ref, conv0_ref, h0_ref, cw_ref, cb_ref, wa_ref, ba_ref, wx_ref,
              bx_ref, lam_ref, o_ref, convp_ref, hp_ref, convs_ref, hs_ref, xp_ref, a_ref, b_ref, hc_ref):
    i = pl.program_id(0)
    tt = xb_ref.shape[0]
    params = (cw_ref, cb_ref, wa_ref, ba_ref, wx_ref, bx_ref, lam_ref)

    @pl.when(i < n_ptiles)
    def _():
        @pl.when(i % tps == 0)
        def _():
            xp_ref[0:SUBLANES, :] = jnp.zeros((SUBLANES, WIDTH), F32)
            hc_ref[...] = jnp.zeros_like(hc_ref)

        xp_ref[SUBLANES:SUBLANES + tt, :] = xb_ref[...]
        y, h_last = _lru_rows(xp_ref, a_ref, b_ref, tt, hc_ref[...], gb_ref[...], *params)
        o_ref[...] = y.astype(BF16)
        hc_ref[...] = h_last
        xp_ref[0:SUBLANES, :] = xp_ref[tt:tt + SUBLANES, :]

        @pl.when(i % tps == tps - 1)
        def _():
            convp_ref[0] = xp_ref[SUBLANES - (CONV_W - 1):SUBLANES, :]
            hp_ref[0] = h_last

    @pl.when(i >= n_ptiles)
    def _():
        for s in range(tt // seg):
            xp_ref[SUBLANES - (CONV_W - 1):SUBLANES, :] = conv0_ref[s]
            xp_ref[SUBLANES:SUBLANES + seg, :] = xb_ref[s * seg:(s + 1) * seg, :]
            y, h_last = _lru_rows(xp_ref, a_ref, b_ref, seg, h0_ref[s:s + 1, :],
                                  gb_ref[s * seg:(s + 1) * seg, :], *params)
            o_ref[s * seg:(s + 1) * seg, :] = y.astype(BF16)
            convs_ref[s] = xp_ref[seg + SUBLANES - (CONV_W - 1):seg + SUBLANES, :]
            hs_ref[s:s + 1, :] = h_last


def _mixer_specs(n_ptiles, tps, spt):
    pseq = lambda i: jnp.minimum(i, n_ptiles - 1) // tps
    stile = lambda i: jnp.maximum(i - n_ptiles, 0)
    return pseq, stile


def _lru(rest, conv0, h0, cw, cb, wa, ba, wx, bx, lam, n_pseq, seq_len, n_sseq, seg):
    t = rest.shape[0]
    tt = TOKEN_TILE
    tps = seq_len // tt
    n_ptiles = n_pseq * tps
    spt = tt // seg
    pseq, stile = _mixer_specs(n_ptiles, tps, spt)
    fixed = lambda i: (0, 0)
    vec = pl.BlockSpec((1, WIDTH), fixed)
    return pl.pallas_call(
        functools.partial(_lru_body, n_ptiles, tps, seg),
        grid=(t // tt,),
        in_specs=[pl.BlockSpec((tt, WIDTH), lambda i: (i, REST_XB // WIDTH)),
                  pl.BlockSpec((tt, WIDTH), lambda i: (i, REST_GB // WIDTH)),
                  pl.BlockSpec((spt, CONV_W - 1, WIDTH), lambda i: (stile(i), 0, 0)),
                  pl.BlockSpec((spt, WIDTH), lambda i: (stile(i), 0)),
                  pl.BlockSpec((CONV_W, WIDTH), fixed), vec,
                  pl.BlockSpec((WIDTH, WIDTH), fixed), vec, pl.BlockSpec((WIDTH, WIDTH), fixed), vec, vec],
        out_specs=[pl.BlockSpec((tt, WIDTH), lambda i: (i, 0)),
                   pl.BlockSpec((1, CONV_W - 1, WIDTH), lambda i: (pseq(i), 0, 0)),
                   pl.BlockSpec((1, 1, WIDTH), lambda i: (pseq(i), 0, 0)),
                   pl.BlockSpec((spt, CONV_W - 1, WIDTH), lambda i: (stile(i), 0, 0)),
                   pl.BlockSpec((spt, WIDTH), lambda i: (stile(i), 0))],
        out_shape=[jax.ShapeDtypeStruct((t, WIDTH), BF16),
                   jax.ShapeDtypeStruct((n_pseq, CONV_W - 1, WIDTH), F32),
                   jax.ShapeDtypeStruct((n_pseq, 1, WIDTH), F32),
                   jax.ShapeDtypeStruct((n_sseq, CONV_W - 1, WIDTH), F32),
                   jax.ShapeDtypeStruct((n_sseq, WIDTH), F32)],
        scratch_shapes=[pltpu.VMEM((tt + SUBLANES, WIDTH), F32), pltpu.VMEM((tt, WIDTH), F32),
                        pltpu.VMEM((tt, WIDTH), F32), pltpu.VMEM((1, WIDTH), F32)],
        compiler_params=_cparams(("arbitrary",)),
        name="rglru",
    )(rest, rest, conv0, h0, cw, cb, wa, ba, wx, bx, lam)


def _ssd_chunk(xc_ref, z_ref, dtr_ref, y_ref, hst_ref, r0, q, dtb_ref, alog_ref, dsk_ref, gn_ref, exp_ref,
               tril_ref, eye_ref):
    xs = xc_ref[r0:r0 + q, 0:WIDTH]
    bm = xc_ref[r0:r0 + q, WIDTH:WIDTH + 2 * SSM_STATE].astype(BF16)
    cm = xc_ref[r0:r0 + q, WIDTH + 2 * SSM_STATE:SSM_CONV_DIM].astype(BF16)
    dt = jax.nn.softplus(dtr_ref[r0:r0 + q, :] + dtb_ref[...])
    da = dt * (-jnp.exp(alog_ref[...]))
    a_cum = _sel_left(tril_ref[0:q, 0:q], da)
    a_exp = _sel_right(a_cum, exp_ref[...])
    dt_exp = _sel_right(dt, exp_ref[...])
    a_cum_t = sum(_dot_nt(eye_ref[...], p) for p in _split3(a_cum))
    a_last = a_cum[q - 1:q, :]
    xdt = xs * dt_exp
    xdtb = xdt.astype(BF16)
    xw = (xdt * jnp.exp(a_exp[q - 1:q, :] - a_exp)).astype(BF16)
    ea = jnp.exp(a_exp)
    causal = _iota((q, q), 0) >= _iota((q, q), 1)
    for g in range(2):
        bg = bm[:, SSM_STATE * g:SSM_STATE * (g + 1)]
        cg = cm[:, SSM_STATE * g:SSM_STATE * (g + 1)]
        cb = _dot_nt(cg, bg)
        for e in range(4 * g, 4 * g + 4):
            hs = slice(HEAD_DIM * e, HEAD_DIM * (e + 1))
            seg = a_cum[:, e:e + 1] - a_cum_t[e:e + 1, :]
            m = (cb * jnp.exp(jnp.where(causal, seg, -1e30))).astype(BF16)
            h_old = hst_ref[e]
            y = _dot(m, xdtb[:, hs]) + _dot_nt(cg, h_old.astype(BF16)) * ea[:, hs]
            y_ref[r0:r0 + q, hs] = y
            decay = jnp.exp(jnp.broadcast_to(a_last[:, e:e + 1], (1, SSM_STATE)))
            hst_ref[e] = decay * h_old + _dot_tn(xw[:, hs], bg)
    y = y_ref[r0:r0 + q, :] + dsk_ref[...] * xs
    y = y * _silu(z_ref[r0:r0 + q, :])
    half = WIDTH // 2
    outs = []
    for g in range(2):
        yg = y[:, half * g:half * (g + 1)]
        outs.append(yg * lax.rsqrt(jnp.mean(yg * yg, axis=-1, keepdims=True) + EPS))
    return jnp.concatenate(outs, axis=1) * gn_ref[...]


def _ssd_body(n_ptiles, tps, seg, q_prompt, xbc_ref, z_ref, dtr_ref, conv0_ref, h0_ref, cw_ref, cb_ref, dtb_ref,
              alog_ref, dsk_ref, gn_ref, exp_ref, tril_ref, eye_ref, o_ref, convp_ref, hp_ref, convs_ref, hs_ref,
              xp_ref, xc_ref, y_ref, hst_ref):
    i = pl.program_id(0)
    tt = xbc_ref.shape[0]
    params = (dtb_ref, alog_ref, dsk_ref, gn_ref, exp_ref, tril_ref, eye_ref)
    tail = slice(SUBLANES - (CONV_W - 1), SUBLANES)

    @pl.when(i < n_ptiles)
    def _():
        @pl.when(i % tps == 0)
        def _():
            xp_ref[0:SUBLANES, :] = jnp.zeros((SUBLANES, SSM_CONV_DIM), F32)
            hst_ref[...] = jnp.zeros_like(hst_ref)

        xp_ref[SUBLANES:SUBLANES + tt, :] = xbc_ref[...]
        xc_ref[...] = _silu(_causal_conv(xp_ref, tt, cw_ref, cb_ref))
        for c in range(tt // q_prompt):
            r0 = c * q_prompt
            o_ref[r0:r0 + q_prompt, :] = _ssd_chunk(xc_ref, z_ref, dtr_ref, y_ref, hst_ref, r0, q_prompt,
                                                    *params).astype(BF16)
        xp_ref[0:SUBLANES, :] = xp_ref[tt:tt + SUBLANES, :]

        @pl.when(i % tps == tps - 1)
        def _():
            convp_ref[0] = xp_ref[tail, :]
            hp_ref[0] = hst_ref[...]

    @pl.when(i >= n_ptiles)
    def _():
        for s in range(tt // seg):
            r0 = s * seg
            xp_ref[tail, :] = conv0_ref[s]
            xp_ref[SUBLANES:SUBLANES + seg, :] = xbc_ref[r0:r0 + seg, :]
            xc_ref[r0:r0 + seg, :] = _silu(_causal_conv(xp_ref, seg, cw_ref, cb_ref))
            hst_ref[...] = h0_ref[s]
            o_ref[r0:r0 + seg, :] = _ssd_chunk(xc_ref, z_ref, dtr_ref, y_ref, hst_ref, r0, seg,
                                               *params).astype(BF16)
            convs_ref[s] = xp_ref[seg + SUBLANES - (CONV_W - 1):seg + SUBLANES, :]
            hs_ref[s] = hst_ref[...]


def _ssd(rest, conv0, h0, cw, cb, dtb, alog, dsk, gn, expand, tril, eye, n_pseq, seq_len, n_sseq, seg):
    t = rest.shape[0]
    tt = TOKEN_TILE
    tps = seq_len // tt
    n_ptiles = n_pseq * tps
    spt = tt // seg
    pseq, stile = _mixer_specs(n_ptiles, tps, spt)
    fixed = lambda i: (0, 0)
    hshape = (N_HEADS, HEAD_DIM, SSM_STATE)
    return pl.pallas_call(
        functools.partial(_ssd_body, n_ptiles, tps, seg, SSD_CHUNK),
        grid=(t // tt,),
        in_specs=[pl.BlockSpec((tt, SSM_CONV_DIM), lambda i: (i, REST_XBC // SSM_CONV_DIM)),
                  pl.BlockSpec((tt, WIDTH), lambda i: (i, REST_ZC // WIDTH)),
                  pl.BlockSpec((tt, LANES), lambda i: (i, REST_DT // LANES)),
                  pl.BlockSpec((spt, CONV_W - 1, SSM_CONV_DIM), lambda i: (stile(i), 0, 0)),
                  pl.BlockSpec((spt,) + hshape, lambda i: (stile(i), 0, 0, 0)),
                  pl.BlockSpec((CONV_W, SSM_CONV_DIM), fixed), pl.BlockSpec((1, SSM_CONV_DIM), fixed),
                  pl.BlockSpec((1, LANES), fixed), pl.BlockSpec((1, LANES), fixed),
                  pl.BlockSpec((1, WIDTH), fixed), pl.BlockSpec((1, WIDTH), fixed),
                  pl.BlockSpec((LANES, WIDTH), fixed), pl.BlockSpec((KEY_BLOCK, KEY_BLOCK), fixed),
                  pl.BlockSpec((SUBLANES, LANES), fixed)],
        out_specs=[pl.BlockSpec((tt, WIDTH), lambda i: (i, 0)),
                   pl.BlockSpec((1, CONV_W - 1, SSM_CONV_DIM), lambda i: (pseq(i), 0, 0)),
                   pl.BlockSpec((1,) + hshape, lambda i: (pseq(i), 0, 0, 0)),
                   pl.BlockSpec((spt, CONV_W - 1, SSM_CONV_DIM), lambda i: (stile(i), 0, 0)),
                   pl.BlockSpec((spt,) + hshape, lambda i: (stile(i), 0, 0, 0))],
        out_shape=[jax.ShapeDtypeStruct((t, WIDTH), BF16),
                   jax.ShapeDtypeStruct((n_pseq, CONV_W - 1, SSM_CONV_DIM), F32),
                   jax.ShapeDtypeStruct((n_pseq,) + hshape, F32),
                   jax.ShapeDtypeStruct((n_sseq, CONV_W - 1, SSM_CONV_DIM), F32),
                   jax.ShapeDtypeStruct((n_sseq,) + hshape, F32)],
        scratch_shapes=[pltpu.VMEM((tt + SUBLANES, SSM_CONV_DIM), F32), pltpu.VMEM((tt, SSM_CONV_DIM), F32),
                        pltpu.VMEM((tt, WIDTH), F32), pltpu.VMEM(hshape, F32)],
        compiler_params=_cparams(("arbitrary",)),
        name="ssd",
    )(rest, rest, rest, conv0, h0, cw, cb, dtb, alog, dsk, gn, expand, tril, eye)


def _route(lt, be_ref, bg_ref):
    n = lt.shape[1]
    le = lt[0:N_EXPERTS, :] + be_ref[:, 0:1]
    lg = lt[N_EXPERTS:N_EXPERTS + SUBLANES, :] + bg_ref[:, 0:1]
    gmax = jnp.max(lg, axis=0, keepdims=True)
    gi = _iota((SUBLANES, n), 0)
    g_sel = jnp.min(jnp.where(lg == gmax, gi, SUBLANES), axis=0, keepdims=True)
    p_sel = 1.0 / jnp.sum(jnp.exp(lg - gmax), axis=0, keepdims=True)
    ei = _iota((N_EXPERTS, n), 0)
    m1 = jnp.where((ei >> 3) == g_sel, le, -jnp.inf)
    v1 = jnp.max(m1, axis=0, keepdims=True)
    i1 = jnp.min(jnp.where(m1 == v1, ei, N_EXPERTS), axis=0, keepdims=True)
    m2 = jnp.where(ei == i1, -jnp.inf, m1)
    v2 = jnp.max(m2, axis=0, keepdims=True)
    i2 = jnp.min(jnp.where(m2 == v2, ei, N_EXPERTS), axis=0, keepdims=True)
    e2 = jnp.exp(v2 - v1)
    w1 = p_sel / (1.0 + e2)
    w2 = w1 * e2
    r = _iota((SUBLANES, n), 0)
    out = jnp.where(r == 0, i1.astype(F32), 0.0)
    out = jnp.where(r == 1, i2.astype(F32), out)
    out = jnp.where(r == 2, w1, out)
    return jnp.where(r == 3, w2, out)


def _merge_body(n_ptiles, h_ref, oap_ref, oas_ref, ob_ref, oc_ref, gmix_ref, wg_ref, bg_ref, wbo_ref, wout_ref,
                gffn_ref, wr_ref, bre_ref, brg_ref, h1_ref, u2_ref, route_ref, ya_ref):
    i = pl.program_id(0)
    x = h_ref[...]
    ub = _rmsnorm(x, gmix_ref[...]).astype(BF16)

    @pl.when(i < n_ptiles)
    def _():
        ya_ref[...] = _dot(oap_ref[...], wbo_ref[0:WIDTH, :])

    @pl.when(i >= n_ptiles)
    def _():
        ya_ref[...] = _dot(oas_ref[...].astype(BF16), wbo_ref[0:WIDTH, :])

    def gate(b):
        return jax.nn.sigmoid(_dot(ub, wg_ref[:, D_MODEL * b:D_MODEL * (b + 1)])
                              + bg_ref[:, D_MODEL * b:D_MODEL * (b + 1)])

    merged = gate(0) * ya_ref[...]
    merged = merged + gate(1) * _dot(ob_ref[...], wbo_ref[WIDTH:2 * WIDTH, :])
    merged = merged + gate(2) * _dot(oc_ref[...], wbo_ref[2 * WIDTH:3 * WIDTH, :])
    h1 = x + _dot(merged.astype(BF16), wout_ref[...])
    h1_ref[...] = h1
    u2 = _rmsnorm(h1, gffn_ref[...])
    u2_ref[...] = u2
    lt = _dot_nt(wr_ref[...].astype(BF16), u2.astype(BF16))
    route_ref[...] = _route(lt, bre_ref, brg_ref)


def _merge(h, oa_p, oa_s, ob, oc, gmix, wg, bg, wbo, wout, gffn, wr, bre, brg):
    t = h.shape[0]
    tm = PROJ_TILE
    n_ptiles = oa_p.shape[0] // tm
    row = lambda i: (i, 0)
    fixed = lambda i: (0, 0)
    full = lambda a: pl.BlockSpec(a.shape, fixed, pipeline_mode=pl.Buffered(1))
    return pl.pallas_call(
        functools.partial(_merge_body, n_ptiles),
        grid=(t // tm,),
        in_specs=[pl.BlockSpec((tm, D_MODEL), row),
                  pl.BlockSpec((tm, WIDTH), lambda i: (jnp.minimum(i, n_ptiles - 1), 0)),
                  pl.BlockSpec((tm, WIDTH), lambda i: (jnp.maximum(i - n_ptiles, 0), 0)),
                  pl.BlockSpec((tm, WIDTH), row), pl.BlockSpec((tm, WIDTH), row),
                  full(gmix), full(wg), full(bg), full(wbo), full(wout), full(gffn), full(wr), full(bre), full(brg)],
        out_specs=[pl.BlockSpec((tm, D_MODEL), row), pl.BlockSpec((tm, D_MODEL), row),
                   pl.BlockSpec((SUBLANES, tm), lambda i: (0, i))],
        out_shape=[jax.ShapeDtypeStruct((t, D_MODEL), F32), jax.ShapeDtypeStruct((t, D_MODEL), F32),
                   jax.ShapeDtypeStruct((SUBLANES, t), F32)],
        scratch_shapes=[pltpu.VMEM((tm, D_MODEL), F32)],
        compiler_params=_cparams(("arbitrary",)),
        name="merge",
    )(h, oa_p, oa_s, ob, oc, gmix, wg, bg, wbo, wout, gffn, wr, bre, brg)


def _rank_body(route_ref, before_ref, ltri_ref, dest_ref, meta_ref, cnt_ref, start_ref):
    p = pl.program_id(0)
    i = pl.program_id(1)
    tm = route_ref.shape[1]
    ei = _iota((N_EXPERTS, tm), 0)
    hot0 = ei == route_ref[0:1, :].astype(I32)
    hot1 = ei == route_ref[1:2, :].astype(I32)
    both = jnp.where(jnp.logical_or(hot0, hot1), 1.0, 0.0)
    tile_cnt = jnp.broadcast_to(jnp.sum(both, axis=1, keepdims=True), (N_EXPERTS, LANES))

    @pl.when(jnp.logical_and(p == 0, i == 0))
    def _():
        cnt_ref[...] = jnp.zeros_like(cnt_ref)

    @pl.when(jnp.logical_and(p == 1, i == 0))
    def _():
        cnt = cnt_ref[...]
        padded = jnp.floor((cnt + (EXPERT_BLOCK - 1)) * (1.0 / EXPERT_BLOCK)) * EXPERT_BLOCK
        start = _sel_left(ltri_ref[...], padded)
        start_ref[...] = start
        lane = _iota((N_EXPERTS, LANES), 1)
        meta_ref[...] = jnp.where(lane == 0, cnt, jnp.where(lane == 1, start, 0.0))
        cnt_ref[...] = jnp.zeros_like(cnt_ref)

    @pl.when(p == 1)
    def _():
        prior = _dot(both.astype(BF16), before_ref[...])
        slot = prior + (start_ref[:, 0:1] + cnt_ref[:, 0:1])
        d0 = jnp.sum(jnp.where(hot0, slot, 0.0), axis=0, keepdims=True)
        d1 = jnp.sum(jnp.where(hot1, slot, 0.0), axis=0, keepdims=True)
        dest_ref[0] = jnp.concatenate([d0, d1], axis=0).astype(I32)

    cnt_ref[...] = cnt_ref[...] + tile_cnt


def _rank(route, before, ltri):
    t = route.shape[1]
    tm = TOKEN_TILE
    nt = t // tm
    return pl.pallas_call(
        _rank_body,
        grid=(2, nt),
        in_specs=[pl.BlockSpec((SUBLANES, tm), lambda p, i: (0, i)),
                  pl.BlockSpec((tm, tm), lambda p, i: (0, 0)),
                  pl.BlockSpec((N_EXPERTS, N_EXPERTS), lambda p, i: (0, 0))],
        out_specs=[pl.BlockSpec((1, 2, tm), lambda p, i: (i * p, 0, 0)),
                   pl.BlockSpec((N_EXPERTS, LANES), lambda p, i: (0, 0))],
        out_shape=[jax.ShapeDtypeStruct((nt, 2, tm), I32), jax.ShapeDtypeStruct((N_EXPERTS, LANES), F32)],
        scratch_shapes=[pltpu.VMEM((N_EXPERTS, LANES), F32), pltpu.VMEM((N_EXPERTS, LANES), F32)],
        compiler_params=_cparams(("arbitrary", "arbitrary")),
        name="moe_rank",
    )(route, before, ltri)


def _row_copy(src_ref, src_row, dst_ref, dst_row, sem):
    return pltpu.make_async_copy(src_ref.at[pl.ds(src_row, 1), :], dst_ref.at[pl.ds(dst_row, 1), :], sem)


def _dispatch_body(dest_ref, u_ref, xs_in_ref, xs_ref, sem):
    del xs_in_ref
    tm = u_ref.shape[0]

    def start(t, c):
        for k in range(2):
            _row_copy(u_ref, t, xs_ref, dest_ref[0, k, t], sem).start()
        return c

    def wait(t, c):
        for k in range(2):
            _row_copy(u_ref, t, xs_ref, dest_ref[0, k, t], sem).wait()
        return c

    lax.fori_loop(0, tm, start, 0, unroll=ROW_DMA_UNROLL)
    lax.fori_loop(0, tm, wait, 0, unroll=ROW_DMA_UNROLL)


def _dispatch(dest, u2, xs_zero):
    t = u2.shape[0]
    tm = dest.shape[2]
    return pl.pallas_call(
        _dispatch_body,
        grid=(t // tm,),
        in_specs=[pl.BlockSpec((1, 2, tm), lambda i: (i, 0, 0), memory_space=pltpu.SMEM),
                  pl.BlockSpec((tm, D_MODEL), lambda i: (i, 0)),
                  pl.BlockSpec(memory_space=pl.ANY)],
        out_specs=pl.BlockSpec(memory_space=pl.ANY),
        out_shape=jax.ShapeDtypeStruct(xs_zero.shape, F32),
        scratch_shapes=[pltpu.SemaphoreType.DMA(())],
        input_output_aliases={2: 0},
        compiler_params=_cparams(("arbitrary",), disable_bounds_checks=True),
        name="moe_dispatch",
    )(dest, u2, xs_zero)


def _experts_body(be_ref, nu_ref, x_ref, wg_ref, wu_ref, wd_ref, y_ref, wgb_ref, wub_ref, wdb_ref):
    i = pl.program_id(0)
    live = i < nu_ref[0]

    @pl.when(jnp.logical_or(i == 0, be_ref[i] != be_ref[jnp.maximum(i - 1, 0)]))
    def _():
        wgb_ref[...] = wg_ref[0, 0].astype(BF16)
        wub_ref[...] = wu_ref[0, 0].astype(BF16)
        wdb_ref[...] = wd_ref[0, 0].astype(BF16)

    @pl.when(live)
    def _():
        xb = x_ref[...].astype(BF16)
        a = _silu(_dot(xb, wgb_ref[...])) * _dot(xb, wub_ref[...])
        y_ref[...] = _dot(a.astype(BF16), wdb_ref[...])

    @pl.when(jnp.logical_not(live))
    def _():
        y_ref[...] = jnp.zeros_like(y_ref)


def _experts(blk_expert, n_used, xs, wg, wu, wd, layer):
    cap = xs.shape[0]
    nb = cap // EXPERT_BLOCK
    live = lambda i, nu: jnp.minimum(i, nu[0] - 1)
    rows = lambda i, be, nu: (live(i, nu), 0)
    wsel = lambda i, be, nu: (layer, be[live(i, nu)], 0, 0)
    return pl.pallas_call(
        _experts_body,
        grid_spec=pltpu.PrefetchScalarGridSpec(
            num_scalar_prefetch=2,
            grid=(nb,),
            in_specs=[pl.BlockSpec((EXPERT_BLOCK, D_MODEL), rows),
                      pl.BlockSpec((1, 1, D_MODEL, D_EXPERT), wsel), pl.BlockSpec((1, 1, D_MODEL, D_EXPERT), wsel),
                      pl.BlockSpec((1, 1, D_EXPERT, D_MODEL), wsel)],
            out_specs=pl.BlockSpec((EXPERT_BLOCK, D_MODEL), lambda i, be, nu: (i, 0)),
            scratch_shapes=[pltpu.VMEM((D_MODEL, D_EXPERT), BF16), pltpu.VMEM((D_MODEL, D_EXPERT), BF16),
                            pltpu.VMEM((D_EXPERT, D_MODEL), BF16)]),
        out_shape=jax.ShapeDtypeStruct((cap, D_MODEL), F32),
        compiler_params=_cparams(("arbitrary",)),
        name="moe_experts",
    )(blk_expert, n_used, xs, wg, wu, wd)


def _combine_body(n_ptiles, per, dest_ref, h_ref, route_ref, g_ref, y_hbm, *rest):
    *o_refs, ybuf_ref, sem = rest
    i = pl.program_id(0)
    tm = h_ref.shape[0]
    base = (i % per) * tm

    def start(t, c):
        for k in range(2):
            _row_copy(y_hbm, dest_ref[0, k, base + t], ybuf_ref.at[k], t, sem).start()
        return c

    def wait(t, c):
        for k in range(2):
            _row_copy(y_hbm, dest_ref[0, k, base + t], ybuf_ref.at[k], t, sem).wait()
        return c

    lax.fori_loop(0, tm, start, 0, unroll=ROW_DMA_UNROLL)
    eye = _iota((tm, tm), 0) == _iota((tm, tm), 1)
    w0 = jnp.sum(jnp.where(eye, route_ref[2:3, :], 0.0), axis=1, keepdims=True)
    w1 = jnp.sum(jnp.where(eye, route_ref[3:4, :], 0.0), axis=1, keepdims=True)
    lax.fori_loop(0, tm, wait, 0, unroll=ROW_DMA_UNROLL)
    h2 = h_ref[...] + (w0 * ybuf_ref[0] + w1 * ybuf_ref[1])
    if len(o_refs) == 1:
        o_refs[0][...] = h2
    else:
        @pl.when(i < n_ptiles)
        def _():
            o_refs[0][...] = _rmsnorm(h2, g_ref[...])

        @pl.when(i >= n_ptiles)
        def _():
            o_refs[1][...] = _rmsnorm(h2, g_ref[...])


def _combine(dest, h1, route, g_final, y, t_p, final):
    t = h1.shape[0]
    tm = PROJ_TILE
    per = dest.shape[2] // tm
    n_ptiles = t_p // tm
    if final:
        out_specs = [pl.BlockSpec((tm, D_MODEL), lambda i: (jnp.minimum(i, n_ptiles - 1), 0)),
                     pl.BlockSpec((tm, D_MODEL), lambda i: (jnp.maximum(i - n_ptiles, 0), 0))]
        out_shape = [jax.ShapeDtypeStruct((t_p, D_MODEL), F32), jax.ShapeDtypeStruct((t - t_p, D_MODEL), F32)]
    else:
        out_specs = pl.BlockSpec((tm, D_MODEL), lambda i: (i, 0))
        out_shape = jax.ShapeDtypeStruct((t, D_MODEL), F32)
    return pl.pallas_call(
        functools.partial(_combine_body, n_ptiles, per),
        grid=(t // tm,),
        in_specs=[pl.BlockSpec((1, 2, dest.shape[2]), lambda i: (i // per, 0, 0), memory_space=pltpu.SMEM),
                  pl.BlockSpec((tm, D_MODEL), lambda i: (i, 0)),
                  pl.BlockSpec((SUBLANES, tm), lambda i: (0, i)),
                  pl.BlockSpec((1, D_MODEL), lambda i: (0, 0)),
                  pl.BlockSpec(memory_space=pl.ANY)],
        out_specs=out_specs,
        out_shape=out_shape,
        scratch_shapes=[pltpu.VMEM((2, tm, D_MODEL), F32), pltpu.SemaphoreType.DMA(())],
        compiler_params=_cparams(("arbitrary",), disable_bounds_checks=True),
        name="moe_combine",
    )(dest, h1, route, g_final, y)


def _constants():
    r = jnp.arange(3 * KEY_BLOCK)
    c = jnp.arange(2 * KEY_BLOCK)
    tri = jnp.where(c[None, :] < KEY_BLOCK, (r[:, None] % KEY_BLOCK) > c[None, :], True).astype(BF16)
    q = jnp.arange(KEY_BLOCK)
    tril = (q[:, None] >= q[None, :]).astype(BF16)
    lane = jnp.arange(WIDTH)
    expand = (jnp.arange(LANES)[:, None] == lane[None, :] // HEAD_DIM).astype(BF16)
    eye = (jnp.arange(SUBLANES)[:, None] == jnp.arange(LANES)[None, :]).astype(BF16)
    t = jnp.arange(TOKEN_TILE)
    before = (t[:, None] < t[None, :]).astype(BF16)
    e = jnp.arange(N_EXPERTS)
    ltri = (e[:, None] > e[None, :]).astype(BF16)
    return tri, tril, expand, eye, before, ltri


def _block_diag(w):
    n, d = w.shape[0], w.shape[1]
    out = jnp.zeros((n, d, n, d), w.dtype)
    out = out.at[jnp.arange(n), :, jnp.arange(n), :].set(w)
    return out.reshape(n * d, n * d)


def _proj_weight(w_in):
    wq = w_in[:, 0:WIDTH].reshape(D_MODEL, N_HEADS, 1, HEAD_DIM)
    half = (jnp.arange(N_HEADS) % 2)[None, :, None, None] == jnp.arange(2)[None, None, :, None]
    wqm = jnp.where(half, wq, 0.0).reshape(D_MODEL, QM_W)
    o = 3 * WIDTH
    xb, gb, zc = w_in[:, o:o + WIDTH], w_in[:, o + WIDTH:o + 2 * WIDTH], w_in[:, o + 2 * WIDTH:o + 3 * WIDTH]
    xbc = w_in[:, o + 3 * WIDTH:o + 3 * WIDTH + SSM_CONV_DIM]
    dt = jnp.pad(w_in[:, o + 3 * WIDTH + SSM_CONV_DIM:], ((0, 0), (0, LANES - N_HEADS)))
    return jnp.concatenate([wqm, w_in[:, WIDTH:3 * WIDTH], xbc, xb, gb, zc, dt], axis=1).astype(BF16)


def _pad_lanes(v):
    return jnp.pad(v, (0, LANES - v.shape[0]))[None, :]


def kernel(x_prompt, x_sample, cache_sb_k, cache_sb_v, state_lru_conv, state_lru_h, state_ssm_conv, state_ssm_h, g_mix, w_in, w_gate, b_gate, w_branch_out, w_out, lru_conv_w, lru_conv_b, lru_w_a, lru_b_a, lru_w_x, lru_b_x, lru_lambda, ssm_conv_w, ssm_conv_b, ssm_dt_bias, ssm_a_log, ssm_d, ssm_norm_g, g_ffn, w_router_group, b_router_group, w_router_expert, b_router_expert, w_expert_gate, w_expert_up, w_expert_down, g_final):
    n_p, s_p, _ = x_prompt.shape
    n_s, s_s, _ = x_sample.shape
    depth = w_in.shape[0]
    past = cache_sb_k.shape[2]
    t_p, t_s = n_p * s_p, n_s * s_s
    t = t_p + t_s
    assert s_p % TOKEN_TILE == 0 and TOKEN_TILE % s_s == 0 and t_s % TOKEN_TILE == 0
    assert s_s % SUBLANES == 0 and past % SAMPLE_WINDOW == 0

    tri, tril, expand, eye, before, ltri = _constants()
    cache_k = jnp.transpose(cache_sb_k, (0, 1, 3, 4, 2))
    cache_v = jnp.transpose(cache_sb_v, (0, 1, 3, 4, 2))
    n_blocks = -(-2 * t // EXPERT_BLOCK) + N_EXPERTS
    xs = jnp.zeros((n_blocks * EXPERT_BLOCK, D_MODEL), F32)

    h = jnp.concatenate([x_prompt.reshape(t_p, D_MODEL), x_sample.reshape(t_s, D_MODEL)], axis=0)
    states = []
    for l in range(depth):
        row = lambda a: a[l][None, :]
        qm, k_p, v_p, k_s, v_s, kb, vb, rest = _inproj(h, row(g_mix), _proj_weight(w_in[l]), n_p, s_p)

        oa_p = _attn_prompt(qm, kb, vb, tri, n_p, s_p)
        acc, rp = _attn_sample(qm, kb, vb, cache_k, cache_v, tri, l, t_p, n_s, s_s)
        if past > SAMPLE_WINDOW:
            older = functools.partial(_attn_older, qm, cache_k, cache_v, tri=tri, layer=l, row0=t_p, n_seq=n_s,
                                      tq=s_s)
            oa_s = lax.cond(jnp.max(rp[:, :N_HEADS]) > LOG_CUTOFF,
                            lambda a, r: older(acc=a, rp=r), lambda a, r: a, acc, rp)
        else:
            oa_s = acc

        ob, lconv_p, lh_p, lconv_s, lh_s = _lru(
            rest, state_lru_conv[l], state_lru_h[l], lru_conv_w[l], row(lru_conv_b),
            _block_diag(lru_w_a[l]).astype(BF16), row(lru_b_a), _block_diag(lru_w_x[l]).astype(BF16),
            row(lru_b_x), row(lru_lambda), n_p, s_p, n_s, s_s)

        oc, sconv_p, sh_p, sconv_s, sh_s = _ssd(
            rest, state_ssm_conv[l], state_ssm_h[l], ssm_conv_w[l], row(ssm_conv_b),
            _pad_lanes(ssm_dt_bias[l]), _pad_lanes(ssm_a_log[l]), jnp.repeat(ssm_d[l], HEAD_DIM)[None, :],
            row(ssm_norm_g), expand, tril, eye, n_p, s_p, n_s, s_s)

        w_r = jnp.concatenate([w_router_expert[l].T, w_router_group[l].T,
                               jnp.zeros((SUBLANES - N_GROUPS, D_MODEL), F32)], axis=0)
        b_re = jnp.broadcast_to(b_router_expert[l][:, None], (N_EXPERTS, LANES))
        b_rg = jnp.broadcast_to(jnp.concatenate([b_router_group[l], jnp.full((SUBLANES - N_GROUPS,), -1e30, F32)])[:, None],
                                (SUBLANES, LANES))
        h1, u2, route = _merge(h, oa_p, oa_s, ob, oc, row(g_mix), w_gate[l].astype(BF16), row(b_gate),
                               w_branch_out[l].astype(BF16), w_out[l].astype(BF16), row(g_ffn), w_r, b_re, b_rg)

        dest, meta = _rank(route, before, ltri)
        counts, starts = meta[:, 0], meta[:, 1]
        ends = starts + jnp.ceil(counts / EXPERT_BLOCK) * EXPERT_BLOCK
        blk_row = (jnp.arange(n_blocks) * EXPERT_BLOCK).astype(F32)
        blk_expert = jnp.minimum(jnp.sum(ends[None, :] <= blk_row[:, None], axis=1), N_EXPERTS - 1).astype(I32)
        n_used = (ends[N_EXPERTS - 1:] / EXPERT_BLOCK).astype(I32)

        xs = _dispatch(dest, u2, xs)
        y = _experts(blk_expert, n_used, xs, w_expert_gate, w_expert_up, w_expert_down, l)
        h = _combine(dest, h1, route, g_final[None, :], y, t_p, final=(l == depth - 1))

        head = lambda a, n, s: a.reshape(n, s, N_HEADS, HEAD_DIM)
        states.append((k_p, v_p, lconv_p, lh_p[:, 0], sconv_p, sh_p,
                       head(k_s, n_s, s_s), head(v_s, n_s, s_s), lconv_s, lh_s, sconv_s, sh_s))

    stacked = [jnp.stack([st[j] for st in states], axis=0) for j in range(12)]
    for j in range(2):
        kt = stacked[j].reshape(depth, n_p, N_HEADS, HEAD_DIM, s_p)
        stacked[j] = jnp.transpose(kt, (0, 1, 4, 2, 3))
    y_p, y_s = h
    return (y_p.reshape(n_p, s_p, D_MODEL), y_s.reshape(n_s, s_s, D_MODEL)) + tuple(stacked)
```

```python
import functools

import jax
import jax.numpy as jnp
from jax import lax
from jax.experimental import pallas as pl
from jax.experimental.pallas import tpu as pltpu

F32 = jnp.float32
BF16 = jnp.bfloat16
I32 = jnp.int32

D_MODEL = 1024
N_HEADS = 8
HEAD_DIM = 64
WIDTH = 512
CONV_W = 4
SSM_STATE = 128
SSM_CONV_DIM = 1024
N_GROUPS = 4
PER_GROUP = 8
N_EXPERTS = 32
D_EXPERT = 512
LRU_C = 8.0
EPS = 1e-6

LANES = 128
SUBLANES = 8
TOKEN_TILE = 512
PROJ_TILE = 512
KEY_BLOCK = 128
SSD_CHUNK = 64
SAMPLE_WINDOW = 512
EXPERT_BLOCK = 512
VMEM_LIMIT = 56 * 1024 * 1024

LOG_CUTOFF = -88.0

REST_XBC, REST_XB, REST_GB, REST_ZC, REST_DT = 0, 1024, 1536, 2048, 2560
REST_W = 2688
QM_W = 1024
PROJ_W = QM_W + 2 * WIDTH + REST_W


def _cparams(sem, **kw):
    return pltpu.CompilerParams(dimension_semantics=sem, vmem_limit_bytes=VMEM_LIMIT, **kw)


ROW_DMA_UNROLL = 8


def _split3(x):
    hi = x.astype(BF16)
    r = x - hi.astype(F32)
    mid = r.astype(BF16)
    lo = (r - mid.astype(F32)).astype(BF16)
    return hi, mid, lo


def _dot(a, b):
    return jnp.dot(a, b, preferred_element_type=F32)


def _dot_nt(a, b):
    return lax.dot_general(a, b, (((1,), (1,)), ((), ())), preferred_element_type=F32)


def _dot_tn(a, b):
    return lax.dot_general(a, b, (((0,), (0,)), ((), ())), preferred_element_type=F32)


def _sel_right(x, m01):
    return sum(_dot(p, m01) for p in _split3(x))


def _sel_left(m01, x):
    return sum(_dot(m01, p) for p in _split3(x))


def _rmsnorm(x, g):
    return x * lax.rsqrt(jnp.mean(x * x, axis=-1, keepdims=True) + EPS) * g


def _log_sigmoid(z):
    return jnp.minimum(z, 0.0) - jnp.log(1.0 + jnp.exp(-jnp.abs(z)))


def _silu(x):
    return x * jax.nn.sigmoid(x)


def _iota(shape, dim):
    return lax.broadcasted_iota(I32, shape, dim)


def _store_heads(ref, x):
    rows = x.shape[0]
    for h in range(N_HEADS):
        ref[pl.ds(h, rows, stride=N_HEADS), :] = x[:, HEAD_DIM * h:HEAD_DIM * (h + 1)]


def _inproj_body(n_ptiles, h_ref, g_ref, w_ref, qm_ref, kp_ref, vp_ref, ks_ref, vs_ref, kb_ref, vb_ref, rest_ref):
    i = pl.program_id(0)
    ub = _rmsnorm(h_ref[...], g_ref[...]).astype(BF16)

    def proj(c0, c1):
        return _dot(ub, w_ref[:, c0:c1])

    for c in range(0, QM_W, 512):
        qm_ref[:, c:c + 512] = proj(c, c + 512).astype(BF16)
    k = proj(QM_W, QM_W + WIDTH)
    kb_ref[...] = k.astype(BF16)
    v = proj(QM_W + WIDTH, QM_W + 2 * WIDTH)
    vb_ref[...] = v.astype(BF16)

    k_t, v_t = k.T, v.T

    @pl.when(i < n_ptiles)
    def _():
        kp_ref[0] = k_t
        vp_ref[0] = v_t

    @pl.when(i >= n_ptiles)
    def _():
        _store_heads(ks_ref, k)
        _store_heads(vs_ref, v)

    base = QM_W + 2 * WIDTH
    for c in range(0, REST_W, 512):
        c1 = min(c + 512, REST_W)
        rest_ref[:, c:c1] = proj(base + c, base + c1)


def _inproj(h, g, w, n_p, s_p):
    t = h.shape[0]
    tm = PROJ_TILE
    t_p = n_p * s_p
    n_ptiles = t_p // tm
    row = lambda i: (i, 0)
    fixed = lambda i: (0, 0)
    tps = s_p // tm
    ptile = lambda i: jnp.minimum(i, n_ptiles - 1)
    prow = pl.BlockSpec((1, WIDTH, tm), lambda i: (ptile(i) // tps, 0, ptile(i) % tps))
    pshape = jax.ShapeDtypeStruct((n_p, WIDTH, s_p), F32)
    srow = pl.BlockSpec((tm * N_HEADS, HEAD_DIM), lambda i: (jnp.maximum(i - n_ptiles, 0), 0))
    heads = lambda n: jax.ShapeDtypeStruct((n * N_HEADS, HEAD_DIM), F32)
    return pl.pallas_call(
        functools.partial(_inproj_body, n_ptiles),
        grid=(t // tm,),
        in_specs=[pl.BlockSpec((tm, D_MODEL), row), pl.BlockSpec((1, D_MODEL), fixed),
                  pl.BlockSpec((D_MODEL, PROJ_W), fixed, pipeline_mode=pl.Buffered(1))],
        out_specs=[pl.BlockSpec((tm, QM_W), row), prow, prow, srow, srow, pl.BlockSpec((tm, WIDTH), row),
                   pl.BlockSpec((tm, WIDTH), row), pl.BlockSpec((tm, REST_W), row)],
        out_shape=[jax.ShapeDtypeStruct((t, QM_W), BF16), pshape, pshape, heads(t - t_p), heads(t - t_p),
                   jax.ShapeDtypeStruct((t, WIDTH), BF16), jax.ShapeDtypeStruct((t, WIDTH), BF16),
                   jax.ShapeDtypeStruct((t, REST_W), F32)],
        compiler_params=_cparams(("arbitrary",)),
        name="inproj",
    )(h, g, w)


def _sb_weights(zs, mask, r_ref, tri_ref):
    kb = zs[0].shape[1]
    log_beta, log_keep = [], []
    for z in zs:
        ls = _log_sigmoid(z)
        lk = ls - z
        log_beta.append(ls)
        log_keep.append(lk if mask is None else jnp.where(mask, lk, 0.0))
    sums = []
    for lk in log_keep:
        if kb == KEY_BLOCK:
            la = _dot(jnp.concatenate(_split3(lk), axis=1), tri_ref[:, 0:KEY_BLOCK])
            sums.append((la, jnp.broadcast_to(la[:, 0:1] + lk[:, 0:1], la.shape)))
        else:
            sums.append((_sel_right(lk, tri_ref[0:kb, 0:kb]),
                         _sel_right(lk, tri_ref[0:kb, KEY_BLOCK:2 * KEY_BLOCK])))
    ws, top = [], None
    for h, (ls, (la, tot)) in enumerate(zip(log_beta, sums)):
        r_old = r_ref[h]
        w = jnp.exp(ls + la + r_old[:, :kb])
        ws.append((w if mask is None else jnp.where(mask, w, 0.0)).astype(BF16))
        r_new = r_old + tot
        r_ref[h] = r_new
        top = r_new if top is None else jnp.maximum(top, r_new)
    return ws, jnp.max(top)


def _sb_block_pairs(qm, kblk, vblk, mask, acc_ref, r_ref, tri_ref):
    pair = lambda x, h: x[:, LANES * (h // 2):LANES * (h // 2 + 1)]
    zs = [_dot_nt(qm[:, LANES * h:LANES * (h + 1)], pair(kblk, h)) for h in range(N_HEADS)]
    ws, rmax = _sb_weights(zs, mask, r_ref, tri_ref)
    low = _iota((qm.shape[0], LANES), 1) < HEAD_DIM
    for p in range(N_HEADS // 2):
        v2 = pair(vblk, 2 * p)
        acc_ref[:, LANES * p:LANES * (p + 1)] += jnp.where(low, _dot(ws[2 * p], v2), _dot(ws[2 * p + 1], v2))
    return rmax


def _sb_block_heads(q_heads, k_of, v_of, mask, acc_ref, r_ref, tri_ref, keys_on_lanes=False):
    qk, wv = (_dot, _dot_nt) if keys_on_lanes else (_dot_nt, _dot)
    zs = [qk(q_heads[h], k_of(h)) for h in range(N_HEADS)]
    ws, rmax = _sb_weights(zs, mask, r_ref, tri_ref)
    for h in range(N_HEADS):
        acc_ref[:, HEAD_DIM * h:HEAD_DIM * (h + 1)] += wv(ws[h], v_of(h))
    return rmax


def _query_heads(qm):
    return [qm[:, LANES * h + HEAD_DIM * (h % 2):LANES * h + HEAD_DIM * (h % 2 + 1)] for h in range(N_HEADS)]


def _cache_heads(c_ref, block):
    return lambda h: c_ref[0, 0, h, :, KEY_BLOCK * block:KEY_BLOCK * (block + 1)].astype(BF16)


def _more_keys(c):
    j, rmax = c
    return jnp.logical_and(j >= 0, rmax > LOG_CUTOFF)


def _attn_prompt_body(qm_ref, kb_ref, vb_ref, tri_ref, o_ref, acc_ref, r_ref):
    i = pl.program_id(1)
    tq = qm_ref.shape[0]
    acc_ref[...] = jnp.zeros_like(acc_ref)
    r_ref[...] = jnp.zeros_like(r_ref)
    qm = qm_ref[...]

    def block(j, mask):
        off = pl.multiple_of(j * KEY_BLOCK, KEY_BLOCK)
        return _sb_block_pairs(qm, kb_ref[pl.ds(off, KEY_BLOCK), :], vb_ref[pl.ds(off, KEY_BLOCK), :],
                               mask, acc_ref, r_ref, tri_ref)

    rmax = block(i, _iota((tq, tq), 1) < _iota((tq, tq), 0))
    lax.while_loop(_more_keys, lambda c: (c[0] - 1, block(c[0], None)), (i - 1, rmax))
    o_ref[...] = acc_ref[...].astype(BF16)


def _attn_prompt(qm, kb, vb, tri, n_seq, seq_len):
    tq = KEY_BLOCK
    nq = seq_len // tq
    return pl.pallas_call(
        _attn_prompt_body,
        grid=(n_seq, nq),
        in_specs=[pl.BlockSpec((tq, QM_W), lambda b, i: (b * nq + i, 0)),
                  pl.BlockSpec((seq_len, WIDTH), lambda b, i: (b, 0)),
                  pl.BlockSpec((seq_len, WIDTH), lambda b, i: (b, 0)),
                  pl.BlockSpec((3 * KEY_BLOCK, 2 * KEY_BLOCK), lambda b, i: (0, 0))],
        out_specs=pl.BlockSpec((tq, WIDTH), lambda b, i: (b * nq + i, 0)),
        out_shape=jax.ShapeDtypeStruct((n_seq * seq_len, WIDTH), BF16),
        scratch_shapes=[pltpu.VMEM((tq, WIDTH), F32), pltpu.VMEM((N_HEADS, tq, LANES), F32)],
        compiler_params=_cparams(("arbitrary", "arbitrary")),
        name="attn_prompt",
    )(qm, kb, vb, tri)


def _pack_r(r_ref, tq):
    lane = _iota((tq, LANES), 1)
    rp = jnp.zeros((tq, LANES), F32)
    for h in range(N_HEADS):
        rp = jnp.where(lane == h, r_ref[h], rp)
    return rp


def _attn_sample_body(qm_ref, kn_ref, vn_ref, ck_ref, cv_ref, tri_ref, acc_out, rp_out, acc_ref, r_ref):
    tq = qm_ref.shape[0]
    acc_ref[...] = jnp.zeros_like(acc_ref)
    r_ref[...] = jnp.zeros_like(r_ref)
    q_heads = _query_heads(qm_ref[...])
    kn, vn = kn_ref[...], vn_ref[...]
    head = lambda x: (lambda h: x[:, HEAD_DIM * h:HEAD_DIM * (h + 1)])
    causal = _iota((tq, tq), 1) < _iota((tq, tq), 0)
    rmax = _sb_block_heads(q_heads, head(kn), head(vn), causal, acc_ref, r_ref, tri_ref)

    for j in reversed(range(ck_ref.shape[4] // KEY_BLOCK)):
        rmax = lax.cond(rmax > LOG_CUTOFF,
                        lambda j=j: _sb_block_heads(q_heads, _cache_heads(ck_ref, j), _cache_heads(cv_ref, j), None,
                                                    acc_ref, r_ref, tri_ref, keys_on_lanes=True),
                        lambda rmax=rmax: rmax)
    acc_out[...] = acc_ref[...]
    rp_out[...] = _pack_r(r_ref, tq)


def _attn_sample(qm, kb, vb, cache_k, cache_v, tri, layer, row0, n_seq, tq):
    past = cache_k.shape[4]
    win = min(SAMPLE_WINDOW, past)
    blk0 = row0 // tq
    cur = lambda b: (blk0 + b, 0)
    cblock = pl.BlockSpec((1, 1, N_HEADS, HEAD_DIM, win), lambda b: (layer, b, 0, 0, past // win - 1))
    return pl.pallas_call(
        _attn_sample_body,
        grid=(n_seq,),
        in_specs=[pl.BlockSpec((tq, QM_W), cur), pl.BlockSpec((tq, WIDTH), cur), pl.BlockSpec((tq, WIDTH), cur),
                  cblock, cblock,
                  pl.BlockSpec((3 * KEY_BLOCK, 2 * KEY_BLOCK), lambda b: (0, 0))],
        out_specs=[pl.BlockSpec((tq, WIDTH), lambda b: (b, 0)), pl.BlockSpec((tq, LANES), lambda b: (b, 0))],
        out_shape=[jax.ShapeDtypeStruct((n_seq * tq, WIDTH), F32), jax.ShapeDtypeStruct((n_seq * tq, LANES), F32)],
        scratch_shapes=[pltpu.VMEM((tq, WIDTH), F32), pltpu.VMEM((N_HEADS, tq, LANES), F32)],
        compiler_params=_cparams(("arbitrary",)),
        name="attn_sample",
    )(qm, kb, vb, cache_k, cache_v, tri)


def _attn_older_body(qm_ref, ck_ref, cv_ref, acc_in, rp_in, tri_ref, acc_out, acc_ref, r_ref):
    s = pl.program_id(1)
    tq = qm_ref.shape[0]

    @pl.when(s == 0)
    def _():
        acc_ref[...] = acc_in[...]
        rp = rp_in[...]
        for h in range(N_HEADS):
            r_ref[h] = jnp.broadcast_to(rp[:, h:h + 1], (tq, LANES))

    rmax = jnp.max(r_ref[0])
    for h in range(1, N_HEADS):
        rmax = jnp.maximum(rmax, jnp.max(r_ref[h]))

    @pl.when(rmax > LOG_CUTOFF)
    def _():
        _sb_block_heads(_query_heads(qm_ref[...]), _cache_heads(ck_ref, 0), _cache_heads(cv_ref, 0), None,
                        acc_ref, r_ref, tri_ref, keys_on_lanes=True)

    @pl.when(s == pl.num_programs(1) - 1)
    def _():
        acc_out[...] = acc_ref[...]


def _attn_older(qm, cache_k, cache_v, acc, rp, tri, layer, row0, n_seq, tq):
    past = cache_k.shape[4]
    win = min(SAMPLE_WINDOW, past)
    nb = (past - win) // KEY_BLOCK
    blk0 = row0 // tq
    cblock = pl.BlockSpec((1, 1, N_HEADS, HEAD_DIM, KEY_BLOCK), lambda b, s: (layer, b, 0, 0, nb - 1 - s))
    return pl.pallas_call(
        _attn_older_body,
        grid=(n_seq, nb),
        in_specs=[pl.BlockSpec((tq, QM_W), lambda b, s: (blk0 + b, 0)), cblock, cblock,
                  pl.BlockSpec((tq, WIDTH), lambda b, s: (b, 0)), pl.BlockSpec((tq, LANES), lambda b, s: (b, 0)),
                  pl.BlockSpec((3 * KEY_BLOCK, 2 * KEY_BLOCK), lambda b, s: (0, 0))],
        out_specs=pl.BlockSpec((tq, WIDTH), lambda b, s: (b, 0)),
        out_shape=jax.ShapeDtypeStruct((n_seq * tq, WIDTH), F32),
        scratch_shapes=[pltpu.VMEM((tq, WIDTH), F32), pltpu.VMEM((N_HEADS, tq, LANES), F32)],
        compiler_params=_cparams(("arbitrary", "arbitrary")),
        name="attn_older",
    )(qm, cache_k, cache_v, acc, rp, tri)


def _causal_conv(xp_ref, n, w_ref, b_ref):
    y = b_ref[...]
    for k in range(CONV_W):
        y = y + xp_ref[pl.ds(SUBLANES - (CONV_W - 1) + k, n), :] * w_ref[k:k + 1, :]
    return y


def _lru_rows(xp_ref, a_ref, b_ref, n, h0, gate, cw_ref, cb_ref, wa_ref, ba_ref, wx_ref, bx_ref, lam_ref):
    xc = _causal_conv(xp_ref, n, cw_ref, cb_ref)
    xcb = xc.astype(BF16)
    r = jax.nn.sigmoid(_dot(xcb, wa_ref[...]) + ba_ref[...])
    ig = jax.nn.sigmoid(_dot(xcb, wx_ref[...]) + bx_ref[...])
    log_a = LRU_C * r * _log_sigmoid(lam_ref[...])
    a_ref[0:n, :] = jnp.exp(log_a)
    th = jnp.tanh(log_a)
    b_ref[0:n, :] = jnp.sqrt(-2.0 * th / (1.0 - th)) * (ig * xc)
    row = _iota((SUBLANES, WIDTH), 0)

    def group(g, h):
        off = pl.multiple_of(g * SUBLANES, SUBLANES)
        a = a_ref[pl.ds(off, SUBLANES), :]
        b = b_ref[pl.ds(off, SUBLANES), :]
        for s in (1, 2, 4):
            a_prev = jnp.where(row >= s, pltpu.roll(a, s, 0), 1.0)
            b_prev = jnp.where(row >= s, pltpu.roll(b, s, 0), 0.0)
            b = b + a * b_prev
            a = a * a_prev
        hs = a * h + b
        b_ref[pl.ds(off, SUBLANES), :] = hs
        return hs[SUBLANES - 1:SUBLANES, :]

    h_last = lax.fori_loop(0, n // SUBLANES, group, h0)
    y = b_ref[0:n, :] * jax.nn.gelu(gate, approximate=True)
    return y, h_last


def _lru_body(n_ptiles, tps, seg, xb_ref, gb_ref, conv0_ref, h0_ref, cw_ref, cb_ref, wa_ref, ba_ref, wx_ref,
              bx_ref, lam_ref, o_ref, convp_ref, hp_ref, convs_ref, hs_ref, xp_ref, a_ref, b_ref, hc_ref):
    i = pl.program_id(0)
    tt = xb_ref.shape[0]
    params = (cw_ref, cb_ref, wa_ref, ba_ref, wx_ref, bx_ref, lam_ref)

    @pl.when(i < n_ptiles)
    def _():
        @pl.when(i % tps == 0)
        def _():
            xp_ref[0:SUBLANES, :] = jnp.zeros((SUBLANES, WIDTH), F32)
            hc_ref[...] = jnp.zeros_like(hc_ref)

        xp_ref[SUBLANES:SUBLANES + tt, :] = xb_ref[...]
        y, h_last = _lru_rows(xp_ref, a_ref, b_ref, tt, hc_ref[...], gb_ref[...], *params)
        o_ref[...] = y.astype(BF16)
        hc_ref[...] = h_last
        xp_ref[0:SUBLANES, :] = xp_ref[tt:tt + SUBLANES, :]

        @pl.when(i % tps == tps - 1)
        def _():
            convp_ref[0] = xp_ref[SUBLANES - (CONV_W - 1):SUBLANES, :]
            hp_ref[0] = h_last

    @pl.when(i >= n_ptiles)
    def _():
        for s in range(tt // seg):
            xp_ref[SUBLANES - (CONV_W - 1):SUBLANES, :] = conv0_ref[s]
            xp_ref[SUBLANES:SUBLANES + seg, :] = xb_ref[s * seg:(s + 1) * seg, :]
            y, h_last = _lru_rows(xp_ref, a_ref, b_ref, seg, h0_ref[s:s + 1, :],
                                  gb_ref[s * seg:(s + 1) * seg, :], *params)
            o_ref[s * seg:(s + 1) * seg, :] = y.astype(BF16)
            convs_ref[s] = xp_ref[seg + SUBLANES - (CONV_W - 1):seg + SUBLANES, :]
            hs_ref[s:s + 1, :] = h_last


def _mixer_specs(n_ptiles, tps, spt):
    pseq = lambda i: jnp.minimum(i, n_ptiles - 1) // tps
    stile = lambda i: jnp.maximum(i - n_ptiles, 0)
    return pseq, stile


def _lru(rest, conv0, h0, cw, cb, wa, ba, wx, bx, lam, n_pseq, seq_len, n_sseq, seg):
    t = rest.shape[0]
    tt = TOKEN_TILE
    tps = seq_len // tt
    n_ptiles = n_pseq * tps
    spt = tt // seg
    pseq, stile = _mixer_specs(n_ptiles, tps, spt)
    fixed = lambda i: (0, 0)
    vec = pl.BlockSpec((1, WIDTH), fixed)
    return pl.pallas_call(
        functools.partial(_lru_body, n_ptiles, tps, seg),
        grid=(t // tt,),
        in_specs=[pl.BlockSpec((tt, WIDTH), lambda i: (i, REST_XB // WIDTH)),
                  pl.BlockSpec((tt, WIDTH), lambda i: (i, REST_GB // WIDTH)),
                  pl.BlockSpec((spt, CONV_W - 1, WIDTH), lambda i: (stile(i), 0, 0)),
                  pl.BlockSpec((spt, WIDTH), lambda i: (stile(i), 0)),
                  pl.BlockSpec((CONV_W, WIDTH), fixed), vec,
                  pl.BlockSpec((WIDTH, WIDTH), fixed), vec, pl.BlockSpec((WIDTH, WIDTH), fixed), vec, vec],
        out_specs=[pl.BlockSpec((tt, WIDTH), lambda i: (i, 0)),
                   pl.BlockSpec((1, CONV_W - 1, WIDTH), lambda i: (pseq(i), 0, 0)),
                   pl.BlockSpec((1, 1, WIDTH), lambda i: (pseq(i), 0, 0)),
                   pl.BlockSpec((spt, CONV_W - 1, WIDTH), lambda i: (stile(i), 0, 0)),
                   pl.BlockSpec((spt, WIDTH), lambda i: (stile(i), 0))],
        out_shape=[jax.ShapeDtypeStruct((t, WIDTH), BF16),
                   jax.ShapeDtypeStruct((n_pseq, CONV_W - 1, WIDTH), F32),
                   jax.ShapeDtypeStruct((n_pseq, 1, WIDTH), F32),
                   jax.ShapeDtypeStruct((n_sseq, CONV_W - 1, WIDTH), F32),
                   jax.ShapeDtypeStruct((n_sseq, WIDTH), F32)],
        scratch_shapes=[pltpu.VMEM((tt + SUBLANES, WIDTH), F32), pltpu.VMEM((tt, WIDTH), F32),
                        pltpu.VMEM((tt, WIDTH), F32), pltpu.VMEM((1, WIDTH), F32)],
        compiler_params=_cparams(("arbitrary",)),
        name="rglru",
    )(rest, rest, conv0, h0, cw, cb, wa, ba, wx, bx, lam)


def _ssd_chunk(xc_ref, z_ref, dtr_ref, y_ref, hst_ref, r0, q, dtb_ref, alog_ref, dsk_ref, gn_ref, exp_ref,
               tril_ref, eye_ref):
    xs = xc_ref[r0:r0 + q, 0:WIDTH]
    bm = xc_ref[r0:r0 + q, WIDTH:WIDTH + 2 * SSM_STATE].astype(BF16)
    cm = xc_ref[r0:r0 + q, WIDTH + 2 * SSM_STATE:SSM_CONV_DIM].astype(BF16)
    dt = jax.nn.softplus(dtr_ref[r0:r0 + q, :] + dtb_ref[...])
    da = dt * (-jnp.exp(alog_ref[...]))
    a_cum = _sel_left(tril_ref[0:q, 0:q], da)
    a_exp = _sel_right(a_cum, exp_ref[...])
    dt_exp = _sel_right(dt, exp_ref[...])
    a_cum_t = sum(_dot_nt(eye_ref[...], p) for p in _split3(a_cum))
    a_last = a_cum[q - 1:q, :]
    xdt = xs * dt_exp
    xdtb = xdt.astype(BF16)
    xw = (xdt * jnp.exp(a_exp[q - 1:q, :] - a_exp)).astype(BF16)
    ea = jnp.exp(a_exp)
    causal = _iota((q, q), 0) >= _iota((q, q), 1)
    for g in range(2):
        bg = bm[:, SSM_STATE * g:SSM_STATE * (g + 1)]
        cg = cm[:, SSM_STATE * g:SSM_STATE * (g + 1)]
        cb = _dot_nt(cg, bg)
        for e in range(4 * g, 4 * g + 4):
            hs = slice(HEAD_DIM * e, HEAD_DIM * (e + 1))
            seg = a_cum[:, e:e + 1] - a_cum_t[e:e + 1, :]
            m = (cb * jnp.exp(jnp.where(causal, seg, -1e30))).astype(BF16)
            h_old = hst_ref[e]
            y = _dot(m, xdtb[:, hs]) + _dot_nt(cg, h_old.astype(BF16)) * ea[:, hs]
            y_ref[r0:r0 + q, hs] = y
            decay = jnp.exp(jnp.broadcast_to(a_last[:, e:e + 1], (1, SSM_STATE)))
            hst_ref[e] = decay * h_old + _dot_tn(xw[:, hs], bg)
    y = y_ref[r0:r0 + q, :] + dsk_ref[...] * xs
    y = y * _silu(z_ref[r0:r0 + q, :])
    half = WIDTH // 2
    outs = []
    for g in range(2):
        yg = y[:, half * g:half * (g + 1)]
        outs.append(yg * lax.rsqrt(jnp.mean(yg * yg, axis=-1, keepdims=True) + EPS))
    return jnp.concatenate(outs, axis=1) * gn_ref[...]


def _ssd_body(n_ptiles, tps, seg, q_prompt, xbc_ref, z_ref, dtr_ref, conv0_ref, h0_ref, cw_ref, cb_ref, dtb_ref,
              alog_ref, dsk_ref, gn_ref, exp_ref, tril_ref, eye_ref, o_ref, convp_ref, hp_ref, convs_ref, hs_ref,
              xp_ref, xc_ref, y_ref, hst_ref):
    i = pl.program_id(0)
    tt = xbc_ref.shape[0]
    params = (dtb_ref, alog_ref, dsk_ref, gn_ref, exp_ref, tril_ref, eye_ref)
    tail = slice(SUBLANES - (CONV_W - 1), SUBLANES)

    @pl.when(i < n_ptiles)
    def _():
        @pl.when(i % tps == 0)
        def _():
            xp_ref[0:SUBLANES, :] = jnp.zeros((SUBLANES, SSM_CONV_DIM), F32)
            hst_ref[...] = jnp.zeros_like(hst_ref)

        xp_ref[SUBLANES:SUBLANES + tt, :] = xbc_ref[...]
        xc_ref[...] = _silu(_causal_conv(xp_ref, tt, cw_ref, cb_ref))
        for c in range(tt // q_prompt):
            r0 = c * q_prompt
            o_ref[r0:r0 + q_prompt, :] = _ssd_chunk(xc_ref, z_ref, dtr_ref, y_ref, hst_ref, r0, q_prompt,
                                                    *params).astype(BF16)
        xp_ref[0:SUBLANES, :] = xp_ref[tt:tt + SUBLANES, :]

        @pl.when(i % tps == tps - 1)
        def _():
            convp_ref[0] = xp_ref[tail, :]
            hp_ref[0] = hst_ref[...]

    @pl.when(i >= n_ptiles)
    def _():
        for s in range(tt // seg):
            r0 = s * seg
            xp_ref[tail, :] = conv0_ref[s]
            xp_ref[SUBLANES:SUBLANES + seg, :] = xbc_ref[r0:r0 + seg, :]
            xc_ref[r0:r0 + seg, :] = _silu(_causal_conv(xp_ref, seg, cw_ref, cb_ref))
            hst_ref[...] = h0_ref[s]
            o_ref[r0:r0 + seg, :] = _ssd_chunk(xc_ref, z_ref, dtr_ref, y_ref, hst_ref, r0, seg,
                                               *params).astype(BF16)
            convs_ref[s] = xp_ref[seg + SUBLANES - (CONV_W - 1):seg + SUBLANES, :]
            hs_ref[s] = hst_ref[...]


def _ssd(rest, conv0, h0, cw, cb, dtb, alog, dsk, gn, expand, tril, eye, n_pseq, seq_len, n_sseq, seg):
    t = rest.shape[0]
    tt = TOKEN_TILE
    tps = seq_len // tt
    n_ptiles = n_pseq * tps
    spt = tt // seg
    pseq, stile = _mixer_specs(n_ptiles, tps, spt)
    fixed = lambda i: (0, 0)
    hshape = (N_HEADS, HEAD_DIM, SSM_STATE)
    return pl.pallas_call(
        functools.partial(_ssd_body, n_ptiles, tps, seg, SSD_CHUNK),
        grid=(t // tt,),
        in_specs=[pl.BlockSpec((tt, SSM_CONV_DIM), lambda i: (i, REST_XBC // SSM_CONV_DIM)),
                  pl.BlockSpec((tt, WIDTH), lambda i: (i, REST_ZC // WIDTH)),
                  pl.BlockSpec((tt, LANES), lambda i: (i, REST_DT // LANES)),
                  pl.BlockSpec((spt, CONV_W - 1, SSM_CONV_DIM), lambda i: (stile(i), 0, 0)),
                  pl.BlockSpec((spt,) + hshape, lambda i: (stile(i), 0, 0, 0)),
                  pl.BlockSpec((CONV_W, SSM_CONV_DIM), fixed), pl.BlockSpec((1, SSM_CONV_DIM), fixed),
                  pl.BlockSpec((1, LANES), fixed), pl.BlockSpec((1, LANES), fixed),
                  pl.BlockSpec((1, WIDTH), fixed), pl.BlockSpec((1, WIDTH), fixed),
                  pl.BlockSpec((LANES, WIDTH), fixed), pl.BlockSpec((KEY_BLOCK, KEY_BLOCK), fixed),
                  pl.BlockSpec((SUBLANES, LANES), fixed)],
        out_specs=[pl.BlockSpec((tt, WIDTH), lambda i: (i, 0)),
                   pl.BlockSpec((1, CONV_W - 1, SSM_CONV_DIM), lambda i: (pseq(i), 0, 0)),
                   pl.BlockSpec((1,) + hshape, lambda i: (pseq(i), 0, 0, 0)),
                   pl.BlockSpec((spt, CONV_W - 1, SSM_CONV_DIM), lambda i: (stile(i), 0, 0)),
                   pl.BlockSpec((spt,) + hshape, lambda i: (stile(i), 0, 0, 0))],
        out_shape=[jax.ShapeDtypeStruct((t, WIDTH), BF16),
                   jax.ShapeDtypeStruct((n_pseq, CONV_W - 1, SSM_CONV_DIM), F32),
                   jax.ShapeDtypeStruct((n_pseq,) + hshape, F32),
                   jax.ShapeDtypeStruct((n_sseq, CONV_W - 1, SSM_CONV_DIM), F32),
                   jax.ShapeDtypeStruct((n_sseq,) + hshape, F32)],
        scratch_shapes=[pltpu.VMEM((tt + SUBLANES, SSM_CONV_DIM), F32), pltpu.VMEM((tt, SSM_CONV_DIM), F32),
                        pltpu.VMEM((tt, WIDTH), F32), pltpu.VMEM(hshape, F32)],
        compiler_params=_cparams(("arbitrary",)),
        name="ssd",
    )(rest, rest, rest, conv0, h0, cw, cb, dtb, alog, dsk, gn, expand, tril, eye)


def _route(lt, be_ref, bg_ref):
    n = lt.shape[1]
    le = lt[0:N_EXPERTS, :] + be_ref[:, 0:1]
    lg = lt[N_EXPERTS:N_EXPERTS + SUBLANES, :] + bg_ref[:, 0:1]
    gmax = jnp.max(lg, axis=0, keepdims=True)
    gi = _iota((SUBLANES, n), 0)
    g_sel = jnp.min(jnp.where(lg == gmax, gi, SUBLANES), axis=0, keepdims=True)
    p_sel = 1.0 / jnp.sum(jnp.exp(lg - gmax), axis=0, keepdims=True)
    ei = _iota((N_EXPERTS, n), 0)
    m1 = jnp.where((ei >> 3) == g_sel, le, -jnp.inf)
    v1 = jnp.max(m1, axis=0, keepdims=True)
    i1 = jnp.min(jnp.where(m1 == v1, ei, N_EXPERTS), axis=0, keepdims=True)
    m2 = jnp.where(ei == i1, -jnp.inf, m1)
    v2 = jnp.max(m2, axis=0, keepdims=True)
    i2 = jnp.min(jnp.where(m2 == v2, ei, N_EXPERTS), axis=0, keepdims=True)
    e2 = jnp.exp(v2 - v1)
    w1 = p_sel / (1.0 + e2)
    w2 = w1 * e2
    r = _iota((SUBLANES, n), 0)
    out = jnp.where(r == 0, i1.astype(F32), 0.0)
    out = jnp.where(r == 1, i2.astype(F32), out)
    out = jnp.where(r == 2, w1, out)
    return jnp.where(r == 3, w2, out)


def _merge_body(n_ptiles, h_ref, oap_ref, oas_ref, ob_ref, oc_ref, gmix_ref, wg_ref, bg_ref, wbo_ref, wout_ref,
                gffn_ref, wr_ref, bre_ref, brg_ref, h1_ref, u2_ref, route_ref, ya_ref):
    i = pl.program_id(0)
    x = h_ref[...]
    ub = _rmsnorm(x, gmix_ref[...]).astype(BF16)

    @pl.when(i < n_ptiles)
    def _():
        ya_ref[...] = _dot(oap_ref[...], wbo_ref[0:WIDTH, :])

    @pl.when(i >= n_ptiles)
    def _():
        ya_ref[...] = _dot(oas_ref[...].astype(BF16), wbo_ref[0:WIDTH, :])

    def gate(b):
        return jax.nn.sigmoid(_dot(ub, wg_ref[:, D_MODEL * b:D_MODEL * (b + 1)])
                              + bg_ref[:, D_MODEL * b:D_MODEL * (b + 1)])

    merged = gate(0) * ya_ref[...]
    merged = merged + gate(1) * _dot(ob_ref[...], wbo_ref[WIDTH:2 * WIDTH, :])
    merged = merged + gate(2) * _dot(oc_ref[...], wbo_ref[2 * WIDTH:3 * WIDTH, :])
    h1 = x + _dot(merged.astype(BF16), wout_ref[...])
    h1_ref[...] = h1
    u2 = _rmsnorm(h1, gffn_ref[...])
    u2_ref[...] = u2
    lt = _dot_nt(wr_ref[...].astype(BF16), u2.astype(BF16))
    route_ref[...] = _route(lt, bre_ref, brg_ref)


def _merge(h, oa_p, oa_s, ob, oc, gmix, wg, bg, wbo, wout, gffn, wr, bre, brg):
    t = h.shape[0]
    tm = PROJ_TILE
    n_ptiles = oa_p.shape[0] // tm
    row = lambda i: (i, 0)
    fixed = lambda i: (0, 0)
    full = lambda a: pl.BlockSpec(a.shape, fixed, pipeline_mode=pl.Buffered(1))
    return pl.pallas_call(
        functools.partial(_merge_body, n_ptiles),
        grid=(t // tm,),
        in_specs=[pl.BlockSpec((tm, D_MODEL), row),
                  pl.BlockSpec((tm, WIDTH), lambda i: (jnp.minimum(i, n_ptiles - 1), 0)),
                  pl.BlockSpec((tm, WIDTH), lambda i: (jnp.maximum(i - n_ptiles, 0), 0)),
                  pl.BlockSpec((tm, WIDTH), row), pl.BlockSpec((tm, WIDTH), row),
                  full(gmix), full(wg), full(bg), full(wbo), full(wout), full(gffn), full(wr), full(bre), full(brg)],
        out_specs=[pl.BlockSpec((tm, D_MODEL), row), pl.BlockSpec((tm, D_MODEL), row),
                   pl.BlockSpec((SUBLANES, tm), lambda i: (0, i))],
        out_shape=[jax.ShapeDtypeStruct((t, D_MODEL), F32), jax.ShapeDtypeStruct((t, D_MODEL), F32),
                   jax.ShapeDtypeStruct((SUBLANES, t), F32)],
        scratch_shapes=[pltpu.VMEM((tm, D_MODEL), F32)],
        compiler_params=_cparams(("arbitrary",)),
        name="merge",
    )(h, oa_p, oa_s, ob, oc, gmix, wg, bg, wbo, wout, gffn, wr, bre, brg)


def _rank_body(route_ref, before_ref, ltri_ref, dest_ref, meta_ref, cnt_ref, start_ref):
    p = pl.program_id(0)
    i = pl.program_id(1)
    tm = route_ref.shape[1]
    ei = _iota((N_EXPERTS, tm), 0)
    hot0 = ei == route_ref[0:1, :].astype(I32)
    hot1 = ei == route_ref[1:2, :].astype(I32)
    both = jnp.where(jnp.logical_or(hot0, hot1), 1.0, 0.0)
    tile_cnt = jnp.broadcast_to(jnp.sum(both, axis=1, keepdims=True), (N_EXPERTS, LANES))

    @pl.when(jnp.logical_and(p == 0, i == 0))
    def _():
        cnt_ref[...] = jnp.zeros_like(cnt_ref)

    @pl.when(jnp.logical_and(p == 1, i == 0))
    def _():
        cnt = cnt_ref[...]
        padded = jnp.floor((cnt + (EXPERT_BLOCK - 1)) * (1.0 / EXPERT_BLOCK)) * EXPERT_BLOCK
        start = _sel_left(ltri_ref[...], padded)
        start_ref[...] = start
        lane = _iota((N_EXPERTS, LANES), 1)
        meta_ref[...] = jnp.where(lane == 0, cnt, jnp.where(lane == 1, start, 0.0))
        cnt_ref[...] = jnp.zeros_like(cnt_ref)

    @pl.when(p == 1)
    def _():
        prior = _dot(both.astype(BF16), before_ref[...])
        slot = prior + (start_ref[:, 0:1] + cnt_ref[:, 0:1])
        d0 = jnp.sum(jnp.where(hot0, slot, 0.0), axis=0, keepdims=True)
        d1 = jnp.sum(jnp.where(hot1, slot, 0.0), axis=0, keepdims=True)
        dest_ref[0] = jnp.concatenate([d0, d1], axis=0).astype(I32)

    cnt_ref[...] = cnt_ref[...] + tile_cnt


def _rank(route, before, ltri):
    t = route.shape[1]
    tm = TOKEN_TILE
    nt = t // tm
    return pl.pallas_call(
        _rank_body,
        grid=(2, nt),
        in_specs=[pl.BlockSpec((SUBLANES, tm), lambda p, i: (0, i)),
                  pl.BlockSpec((tm, tm), lambda p, i: (0, 0)),
                  pl.BlockSpec((N_EXPERTS, N_EXPERTS), lambda p, i: (0, 0))],
        out_specs=[pl.BlockSpec((1, 2, tm), lambda p, i: (i * p, 0, 0)),
                   pl.BlockSpec((N_EXPERTS, LANES), lambda p, i: (0, 0))],
        out_shape=[jax.ShapeDtypeStruct((nt, 2, tm), I32), jax.ShapeDtypeStruct((N_EXPERTS, LANES), F32)],
        scratch_shapes=[pltpu.VMEM((N_EXPERTS, LANES), F32), pltpu.VMEM((N_EXPERTS, LANES), F32)],
        compiler_params=_cparams(("arbitrary", "arbitrary")),
        name="moe_rank",
    )(route, before, ltri)


def _row_copy(src_ref, src_row, dst_ref, dst_row, sem):
    return pltpu.make_async_copy(src_ref.at[pl.ds(src_row, 1), :], dst_ref.at[pl.ds(dst_row, 1), :], sem)


def _dispatch_body(dest_ref, u_ref, xs_in_ref, xs_ref, sem):
    del xs_in_ref
    tm = u_ref.shape[0]

    def start(t, c):
        for k in range(2):
            _row_copy(u_ref, t, xs_ref, dest_ref[0, k, t], sem).start()
        return c

    def wait(t, c):
        for k in range(2):
            _row_copy(u_ref, t, xs_ref, dest_ref[0, k, t], sem).wait()
        return c

    lax.fori_loop(0, tm, start, 0, unroll=ROW_DMA_UNROLL)
    lax.fori_loop(0, tm, wait, 0, unroll=ROW_DMA_UNROLL)


def _dispatch(dest, u2, xs_zero):
    t = u2.shape[0]
    tm = dest.shape[2]
    return pl.pallas_call(
        _dispatch_body,
        grid=(t // tm,),
        in_specs=[pl.BlockSpec((1, 2, tm), lambda i: (i, 0, 0), memory_space=pltpu.SMEM),
                  pl.BlockSpec((tm, D_MODEL), lambda i: (i, 0)),
                  pl.BlockSpec(memory_space=pl.ANY)],
        out_specs=pl.BlockSpec(memory_space=pl.ANY),
        out_shape=jax.ShapeDtypeStruct(xs_zero.shape, F32),
        scratch_shapes=[pltpu.SemaphoreType.DMA(())],
        input_output_aliases={2: 0},
        compiler_params=_cparams(("arbitrary",), disable_bounds_checks=True),
        name="moe_dispatch",
    )(dest, u2, xs_zero)


def _experts_body(be_ref, nu_ref, x_ref, wg_ref, wu_ref, wd_ref, y_ref, wgb_ref, wub_ref, wdb_ref):
    i = pl.program_id(0)
    live = i < nu_ref[0]

    @pl.when(jnp.logical_or(i == 0, be_ref[i] != be_ref[jnp.maximum(i - 1, 0)]))
    def _():
        wgb_ref[...] = wg_ref[0, 0].astype(BF16)
        wub_ref[...] = wu_ref[0, 0].astype(BF16)
        wdb_ref[...] = wd_ref[0, 0].astype(BF16)

    @pl.when(live)
    def _():
        xb = x_ref[...].astype(BF16)
        a = _silu(_dot(xb, wgb_ref[...])) * _dot(xb, wub_ref[...])
        y_ref[...] = _dot(a.astype(BF16), wdb_ref[...])

    @pl.when(jnp.logical_not(live))
    def _():
        y_ref[...] = jnp.zeros_like(y_ref)


def _experts(blk_expert, n_used, xs, wg, wu, wd, layer):
    cap = xs.shape[0]
    nb = cap // EXPERT_BLOCK
    live = lambda i, nu: jnp.minimum(i, nu[0] - 1)
    rows = lambda i, be, nu: (live(i, nu), 0)
    wsel = lambda i, be, nu: (layer, be[live(i, nu)], 0, 0)
    return pl.pallas_call(
        _experts_body,
        grid_spec=pltpu.PrefetchScalarGridSpec(
            num_scalar_prefetch=2,
            grid=(nb,),
            in_specs=[pl.BlockSpec((EXPERT_BLOCK, D_MODEL), rows),
                      pl.BlockSpec((1, 1, D_MODEL, D_EXPERT), wsel), pl.BlockSpec((1, 1, D_MODEL, D_EXPERT), wsel),
                      pl.BlockSpec((1, 1, D_EXPERT, D_MODEL), wsel)],
            out_specs=pl.BlockSpec((EXPERT_BLOCK, D_MODEL), lambda i, be, nu: (i, 0)),
            scratch_shapes=[pltpu.VMEM((D_MODEL, D_EXPERT), BF16), pltpu.VMEM((D_MODEL, D_EXPERT), BF16),
                            pltpu.VMEM((D_EXPERT, D_MODEL), BF16)]),
        out_shape=jax.ShapeDtypeStruct((cap, D_MODEL), F32),
        compiler_params=_cparams(("arbitrary",)),
        name="moe_experts",
    )(blk_expert, n_used, xs, wg, wu, wd)


def _combine_body(n_ptiles, per, dest_ref, h_ref, route_ref, g_ref, y_hbm, *rest):
    *o_refs, ybuf_ref, sem = rest
    i = pl.program_id(0)
    tm = h_ref.shape[0]
    base = (i % per) * tm

    def start(t, c):
        for k in range(2):
            _row_copy(y_hbm, dest_ref[0, k, base + t], ybuf_ref.at[k], t, sem).start()
        return c

    def wait(t, c):
        for k in range(2):
            _row_copy(y_hbm, dest_ref[0, k, base + t], ybuf_ref.at[k], t, sem).wait()
        return c

    lax.fori_loop(0, tm, start, 0, unroll=ROW_DMA_UNROLL)
    eye = _iota((tm, tm), 0) == _iota((tm, tm), 1)
    w0 = jnp.sum(jnp.where(eye, route_ref[2:3, :], 0.0), axis=1, keepdims=True)
    w1 = jnp.sum(jnp.where(eye, route_ref[3:4, :], 0.0), axis=1, keepdims=True)
    lax.fori_loop(0, tm, wait, 0, unroll=ROW_DMA_UNROLL)
    h2 = h_ref[...] + (w0 * ybuf_ref[0] + w1 * ybuf_ref[1])
    if len(o_refs) == 1:
        o_refs[0][...] = h2
    else:
        @pl.when(i < n_ptiles)
        def _():
            o_refs[0][...] = _rmsnorm(h2, g_ref[...])

        @pl.when(i >= n_ptiles)
        def _():
            o_refs[1][...] = _rmsnorm(h2, g_ref[...])


def _combine(dest, h1, route, g_final, y, t_p, final):
    t = h1.shape[0]
    tm = PROJ_TILE
    per = dest.shape[2] // tm
    n_ptiles = t_p // tm
    if final:
        out_specs = [pl.BlockSpec((tm, D_MODEL), lambda i: (jnp.minimum(i, n_ptiles - 1), 0)),
                     pl.BlockSpec((tm, D_MODEL), lambda i: (jnp.maximum(i - n_ptiles, 0), 0))]
        out_shape = [jax.ShapeDtypeStruct((t_p, D_MODEL), F32), jax.ShapeDtypeStruct((t - t_p, D_MODEL), F32)]
    else:
        out_specs = pl.BlockSpec((tm, D_MODEL), lambda i: (i, 0))
        out_shape = jax.ShapeDtypeStruct((t, D_MODEL), F32)
    return pl.pallas_call(
        functools.partial(_combine_body, n_ptiles, per),
        grid=(t // tm,),
        in_specs=[pl.BlockSpec((1, 2, dest.shape[2]), lambda i: (i // per, 0, 0), memory_space=pltpu.SMEM),
                  pl.BlockSpec((tm, D_MODEL), lambda i: (i, 0)),
                  pl.BlockSpec((SUBLANES, tm), lambda i: (0, i)),
                  pl.BlockSpec((1, D_MODEL), lambda i: (0, 0)),
                  pl.BlockSpec(memory_space=pl.ANY)],
        out_specs=out_specs,
        out_shape=out_shape,
        scratch_shapes=[pltpu.VMEM((2, tm, D_MODEL), F32), pltpu.SemaphoreType.DMA(())],
        compiler_params=_cparams(("arbitrary",), disable_bounds_checks=True),
        name="moe_combine",
    )(dest, h1, route, g_final, y)


def _constants():
    r = jnp.arange(3 * KEY_BLOCK)
    c = jnp.arange(2 * KEY_BLOCK)
    tri = jnp.where(c[None, :] < KEY_BLOCK, (r[:, None] % KEY_BLOCK) > c[None, :], True).astype(BF16)
    q = jnp.arange(KEY_BLOCK)
    tril = (q[:, None] >= q[None, :]).astype(BF16)
    lane = jnp.arange(WIDTH)
    expand = (jnp.arange(LANES)[:, None] == lane[None, :] // HEAD_DIM).astype(BF16)
    eye = (jnp.arange(SUBLANES)[:, None] == jnp.arange(LANES)[None, :]).astype(BF16)
    t = jnp.arange(TOKEN_TILE)
    before = (t[:, None] < t[None, :]).astype(BF16)
    e = jnp.arange(N_EXPERTS)
    ltri = (e[:, None] > e[None, :]).astype(BF16)
    return tri, tril, expand, eye, before, ltri


def _block_diag(w):
    n, d = w.shape[0], w.shape[1]
    out = jnp.zeros((n, d, n, d), w.dtype)
    out = out.at[jnp.arange(n), :, jnp.arange(n), :].set(w)
    return out.reshape(n * d, n * d)


def _proj_weight(w_in):
    wq = w_in[:, 0:WIDTH].reshape(D_MODEL, N_HEADS, 1, HEAD_DIM)
    half = (jnp.arange(N_HEADS) % 2)[None, :, None, None] == jnp.arange(2)[None, None, :, None]
    wqm = jnp.where(half, wq * (HEAD_DIM ** -0.5), 0.0).reshape(D_MODEL, QM_W)
    o = 3 * WIDTH
    xb, gb, zc = w_in[:, o:o + WIDTH], w_in[:, o + WIDTH:o + 2 * WIDTH], w_in[:, o + 2 * WIDTH:o + 3 * WIDTH]
    xbc = w_in[:, o + 3 * WIDTH:o + 3 * WIDTH + SSM_CONV_DIM]
    dt = jnp.pad(w_in[:, o + 3 * WIDTH + SSM_CONV_DIM:], ((0, 0), (0, LANES - N_HEADS)))
    return jnp.concatenate([wqm, w_in[:, WIDTH:3 * WIDTH], xbc, xb, gb, zc, dt], axis=1).astype(BF16)


def _pad_lanes(v):
    return jnp.pad(v, (0, LANES - v.shape[0]))[None, :]


def kernel(x_prompt, x_sample, cache_sb_k, cache_sb_v, state_lru_conv, state_lru_h, state_ssm_conv, state_ssm_h, g_mix, w_in, w_gate, b_gate, w_branch_out, w_out, lru_conv_w, lru_conv_b, lru_w_a, lru_b_a, lru_w_x, lru_b_x, lru_lambda, ssm_conv_w, ssm_conv_b, ssm_dt_bias, ssm_a_log, ssm_d, ssm_norm_g, g_ffn, w_router_group, b_router_group, w_router_expert, b_router_expert, w_expert_gate, w_expert_up, w_expert_down, g_final):
    n_p, s_p, _ = x_prompt.shape
    n_s, s_s, _ = x_sample.shape
    depth = w_in.shape[0]
    past = cache_sb_k.shape[2]
    t_p, t_s = n_p * s_p, n_s * s_s
    t = t_p + t_s
    assert s_p % TOKEN_TILE == 0 and TOKEN_TILE % s_s == 0 and t_s % TOKEN_TILE == 0
    assert s_s % SUBLANES == 0 and past % SAMPLE_WINDOW == 0

    tri, tril, expand, eye, before, ltri = _constants()
    cache_k = jnp.transpose(cache_sb_k, (0, 1, 3, 4, 2))
    cache_v = jnp.transpose(cache_sb_v, (0, 1, 3, 4, 2))
    n_blocks = -(-2 * t // EXPERT_BLOCK) + N_EXPERTS
    xs = jnp.zeros((n_blocks * EXPERT_BLOCK, D_MODEL), F32)

    h = jnp.concatenate([x_prompt.reshape(t_p, D_MODEL), x_sample.reshape(t_s, D_MODEL)], axis=0)
    states = []
    for l in range(depth):
        row = lambda a: a[l][None, :]
        qm, k_p, v_p, k_s, v_s, kb, vb, rest = _inproj(h, row(g_mix), _proj_weight(w_in[l]), n_p, s_p)

        oa_p = _attn_prompt(qm, kb, vb, tri, n_p, s_p)
        acc, rp = _attn_sample(qm, kb, vb, cache_k, cache_v, tri, l, t_p, n_s, s_s)
        if past > SAMPLE_WINDOW:
            older = functools.partial(_attn_older, qm, cache_k, cache_v, tri=tri, layer=l, row0=t_p, n_seq=n_s,
                                      tq=s_s)
            oa_s = lax.cond(jnp.max(rp[:, :N_HEADS]) > LOG_CUTOFF,
                            lambda a, r: older(acc=a, rp=r), lambda a, r: a, acc, rp)
        else:
            oa_s = acc

        ob, lconv_p, lh_p, lconv_s, lh_s = _lru(
            rest, state_lru_conv[l], state_lru_h[l], lru_conv_w[l], row(lru_conv_b),
            _block_diag(lru_w_a[l]).astype(BF16), row(lru_b_a), _block_diag(lru_w_x[l]).astype(BF16),
            row(lru_b_x), row(lru_lambda), n_p, s_p, n_s, s_s)

        oc, sconv_p, sh_p, sconv_s, sh_s = _ssd(
            rest, state_ssm_conv[l], state_ssm_h[l], ssm_conv_w[l], row(ssm_conv_b),
            _pad_lanes(ssm_dt_bias[l]), _pad_lanes(ssm_a_log[l]), jnp.repeat(ssm_d[l], HEAD_DIM)[None, :],
            row(ssm_norm_g), expand, tril, eye, n_p, s_p, n_s, s_s)

        w_r = jnp.concatenate([w_router_expert[l].T, w_router_group[l].T,
                               jnp.zeros((SUBLANES - N_GROUPS, D_MODEL), F32)], axis=0)
        b_re = jnp.broadcast_to(b_router_expert[l][:, None], (N_EXPERTS, LANES))
        b_rg = jnp.broadcast_to(jnp.concatenate([b_router_group[l], jnp.full((SUBLANES - N_GROUPS,), -1e30, F32)])[:, None],
                                (SUBLANES, LANES))
        h1, u2, route = _merge(h, oa_p, oa_s, ob, oc, row(g_mix), w_gate[l].astype(BF16), row(b_gate),
                               w_branch_out[l].astype(BF16), w_out[l].astype(BF16), row(g_ffn), w_r, b_re, b_rg)

        dest, meta = _rank(route, before, ltri)
        counts, starts = meta[:, 0], meta[:, 1]
        ends = starts + jnp.ceil(counts / EXPERT_BLOCK) * EXPERT_BLOCK
        blk_row = (jnp.arange(n_blocks) * EXPERT_BLOCK).astype(F32)
        blk_expert = jnp.minimum(jnp.sum(ends[None, :] <= blk_row[:, None], axis=1), N_EXPERTS - 1).astype(I32)
        n_used = (ends[N_EXPERTS - 1:] / EXPERT_BLOCK).astype(I32)

        xs = _dispatch(dest, u2, xs)
        y = _experts(blk_expert, n_used, xs, w_expert_gate, w_expert_up, w_expert_down, l)
        h = _combine(dest, h1, route, g_final[None, :], y, t_p, final=(l == depth - 1))

        head = lambda a, n, s: a.reshape(n, s, N_HEADS, HEAD_DIM)
        states.append((k_p, v_p, lconv_p, lh_p[:, 0], sconv_p, sh_p,
                       head(k_s, n_s, s_s), head(v_s, n_s, s_s), lconv_s, lh_s, sconv_s, sh_s))

    stacked = [jnp.stack([st[j] for st in states], axis=0) for j in range(12)]
    for j in range(2):
        kt = stacked[j].reshape(depth, n_p, N_HEADS, HEAD_DIM, s_p)
        stacked[j] = jnp.transpose(kt, (0, 1, 4, 2, 3))
    y_p, y_s = h
    return (y_p.reshape(n_p, s_p, D_MODEL), y_s.reshape(n_s, s_s, D_MODEL)) + tuple(stacked)
```

```python
import functools

import jax
import jax.numpy as jnp
from jax import lax
from jax.experimental import pallas as pl
from jax.experimental.pallas import tpu as pltpu

F32 = jnp.float32
BF16 = jnp.bfloat16
I32 = jnp.int32

D_MODEL = 1024
N_HEADS = 8
HEAD_DIM = 64
WIDTH = 512
CONV_W = 4
SSM_STATE = 128
SSM_CONV_DIM = 1024
N_GROUPS = 4
PER_GROUP = 8
N_EXPERTS = 32
D_EXPERT = 512
LRU_C = 8.0
EPS = 1e-6

LANES = 128
SUBLANES = 8
TOKEN_TILE = 512
PROJ_TILE = 512
KEY_BLOCK = 128
SSD_CHUNK = 64
SAMPLE_WINDOW = 512
EXPERT_BLOCK = 512
VMEM_LIMIT = 56 * 1024 * 1024

LOG_CUTOFF = -88.0

REST_XBC, REST_XB, REST_GB, REST_ZC, REST_DT = 0, 1024, 1536, 2048, 2560
REST_W = 2688
QM_W = 1024
PROJ_W = QM_W + 2 * WIDTH + REST_W


def _cparams(sem, **kw):
    return pltpu.CompilerParams(dimension_semantics=sem, vmem_limit_bytes=VMEM_LIMIT, **kw)


ROW_DMA_UNROLL = 8


def _split3(x):
    hi = x.astype(BF16)
    r = x - hi.astype(F32)
    mid = r.astype(BF16)
    lo = (r - mid.astype(F32)).astype(BF16)
    return hi, mid, lo


def _dot(a, b):
    return jnp.dot(a, b, preferred_element_type=F32)


def _dot_nt(a, b):
    return lax.dot_general(a, b, (((1,), (1,)), ((), ())), preferred_element_type=F32)


def _dot_tn(a, b):
    return lax.dot_general(a, b, (((0,), (0,)), ((), ())), preferred_element_type=F32)


def _sel_right(x, m01):
    return sum(_dot(p, m01) for p in _split3(x))


def _sel_left(m01, x):
    return sum(_dot(m01, p) for p in _split3(x))


def _rmsnorm(x, g):
    return x * lax.rsqrt(jnp.mean(x * x, axis=-1, keepdims=True) + EPS) * g


def _log_sigmoid(z):
    return jnp.minimum(z, 0.0) - jnp.log(1.0 + jnp.exp(-jnp.abs(z)))


def _silu(x):
    return x * jax.nn.sigmoid(x)


def _iota(shape, dim):
    return lax.broadcasted_iota(I32, shape, dim)


def _store_heads(ref, x):
    rows = x.shape[0]
    for h in range(N_HEADS):
        ref[pl.ds(h, rows, stride=N_HEADS), :] = x[:, HEAD_DIM * h:HEAD_DIM * (h + 1)]


def _inproj_body(n_ptiles, h_ref, g_ref, w_ref, qm_ref, kp_ref, vp_ref, ks_ref, vs_ref, kb_ref, vb_ref, rest_ref):
    i = pl.program_id(0)
    ub = _rmsnorm(h_ref[...], g_ref[...]).astype(BF16)

    def proj(c0, c1):
        return _dot(ub, w_ref[:, c0:c1])

    for c in range(0, QM_W, 512):
        qm_ref[:, c:c + 512] = proj(c, c + 512).astype(BF16)
    k = proj(QM_W, QM_W + WIDTH)
    kb_ref[...] = k.astype(BF16)
    v = proj(QM_W + WIDTH, QM_W + 2 * WIDTH)
    vb_ref[...] = v.astype(BF16)

    k_t, v_t = k.T, v.T

    @pl.when(i < n_ptiles)
    def _():
        kp_ref[0] = k_t
        vp_ref[0] = v_t

    @pl.when(i >= n_ptiles)
    def _():
        _store_heads(ks_ref, k)
        _store_heads(vs_ref, v)

    base = QM_W + 2 * WIDTH
    for c in range(0, REST_W, 512):
        c1 = min(c + 512, REST_W)
        rest_ref[:, c:c1] = proj(base + c, base + c1)


def _inproj(h, g, w, n_p, s_p):
    t = h.shape[0]
    tm = PROJ_TILE
    t_p = n_p * s_p
    n_ptiles = t_p // tm
    row = lambda i: (i, 0)
    fixed = lambda i: (0, 0)
    tps = s_p // tm
    ptile = lambda i: jnp.minimum(i, n_ptiles - 1)
    prow = pl.BlockSpec((1, WIDTH, tm), lambda i: (ptile(i) // tps, 0, ptile(i) % tps))
    pshape = jax.ShapeDtypeStruct((n_p, WIDTH, s_p), F32)
    srow = pl.BlockSpec((tm * N_HEADS, HEAD_DIM), lambda i: (jnp.maximum(i - n_ptiles, 0), 0))
    heads = lambda n: jax.ShapeDtypeStruct((n * N_HEADS, HEAD_DIM), F32)
    return pl.pallas_call(
        functools.partial(_inproj_body, n_ptiles),
        grid=(t // tm,),
        in_specs=[pl.BlockSpec((tm, D_MODEL), row), pl.BlockSpec((1, D_MODEL), fixed),
                  pl.BlockSpec((D_MODEL, PROJ_W), fixed, pipeline_mode=pl.Buffered(1))],
        out_specs=[pl.BlockSpec((tm, QM_W), row), prow, prow, srow, srow, pl.BlockSpec((tm, WIDTH), row),
                   pl.BlockSpec((tm, WIDTH), row), pl.BlockSpec((tm, REST_W), row)],
        out_shape=[jax.ShapeDtypeStruct((t, QM_W), BF16), pshape, pshape, heads(t - t_p), heads(t - t_p),
                   jax.ShapeDtypeStruct((t, WIDTH), BF16), jax.ShapeDtypeStruct((t, WIDTH), BF16),
                   jax.ShapeDtypeStruct((t, REST_W), F32)],
        compiler_params=_cparams(("arbitrary",)),
        name="inproj",
    )(h, g, w)


def _sb_weights(zs, mask, r_ref, tri_ref):
    kb = zs[0].shape[1]
    log_beta, log_keep = [], []
    for z in zs:
        ls = _log_sigmoid(z)
        lk = ls - z
        log_beta.append(ls)
        log_keep.append(lk if mask is None else jnp.where(mask, lk, 0.0))
    sums = []
    for lk in log_keep:
        if kb == KEY_BLOCK:
            la = _dot(jnp.concatenate(_split3(lk), axis=1), tri_ref[:, 0:KEY_BLOCK])
            sums.append((la, jnp.broadcast_to(la[:, 0:1] + lk[:, 0:1], la.shape)))
        else:
            sums.append((_sel_right(lk, tri_ref[0:kb, 0:kb]),
                         _sel_right(lk, tri_ref[0:kb, KEY_BLOCK:2 * KEY_BLOCK])))
    ws, top = [], None
    for h, (ls, (la, tot)) in enumerate(zip(log_beta, sums)):
        r_old = r_ref[h]
        w = jnp.exp(ls + la + r_old[:, :kb])
        ws.append((w if mask is None else jnp.where(mask, w, 0.0)).astype(BF16))
        r_new = r_old + tot
        r_ref[h] = r_new
        top = r_new if top is None else jnp.maximum(top, r_new)
    return ws, jnp.max(top)


def _sb_block_pairs(qm, kblk, vblk, mask, acc_ref, r_ref, tri_ref):
    pair = lambda x, h: x[:, LANES * (h // 2):LANES * (h // 2 + 1)]
    zs = [_dot_nt(qm[:, LANES * h:LANES * (h + 1)], pair(kblk, h)) for h in range(N_HEADS)]
    ws, rmax = _sb_weights(zs, mask, r_ref, tri_ref)
    low = _iota((qm.shape[0], LANES), 1) < HEAD_DIM
    for p in range(N_HEADS // 2):
        v2 = pair(vblk, 2 * p)
        acc_ref[:, LANES * p:LANES * (p + 1)] += jnp.where(low, _dot(ws[2 * p], v2), _dot(ws[2 * p + 1], v2))
    return rmax


def _sb_block_heads(q_heads, k_of, v_of, mask, acc_ref, r_ref, tri_ref, keys_on_lanes=False):
    qk, wv = (_dot, _dot_nt) if keys_on_lanes else (_dot_nt, _dot)
    zs = [qk(q_heads[h], k_of(h)) for h in range(N_HEADS)]
    ws, rmax = _sb_weights(zs, mask, r_ref, tri_ref)
    for h in range(N_HEADS):
        acc_ref[:, HEAD_DIM * h:HEAD_DIM * (h + 1)] += wv(ws[h], v_of(h))
    return rmax


def _query_heads(qm):
    return [qm[:, LANES * h + HEAD_DIM * (h % 2):LANES * h + HEAD_DIM * (h % 2 + 1)] for h in range(N_HEADS)]


def _cache_heads(c_ref, block):
    return lambda h: c_ref[0, 0, h, :, KEY_BLOCK * block:KEY_BLOCK * (block + 1)].astype(BF16)


def _more_keys(c):
    j, rmax = c
    return jnp.logical_and(j >= 0, rmax > LOG_CUTOFF)


def _attn_prompt_body(qm_ref, kb_ref, vb_ref, tri_ref, o_ref, acc_ref, r_ref):
    i = pl.program_id(1)
    tq = qm_ref.shape[0]
    acc_ref[...] = jnp.zeros_like(acc_ref)
    r_ref[...] = jnp.zeros_like(r_ref)
    qm = qm_ref[...]

    def block(j, mask):
        off = pl.multiple_of(j * KEY_BLOCK, KEY_BLOCK)
        return _sb_block_pairs(qm, kb_ref[pl.ds(off, KEY_BLOCK), :], vb_ref[pl.ds(off, KEY_BLOCK), :],
                               mask, acc_ref, r_ref, tri_ref)

    rmax = block(i, _iota((tq, tq), 1) < _iota((tq, tq), 0))
    lax.while_loop(_more_keys, lambda c: (c[0] - 1, block(c[0], None)), (i - 1, rmax))
    o_ref[...] = acc_ref[...].astype(BF16)


def _attn_prompt(qm, kb, vb, tri, n_seq, seq_len):
    tq = KEY_BLOCK
    nq = seq_len // tq
    return pl.pallas_call(
        _attn_prompt_body,
        grid=(n_seq, nq),
        in_specs=[pl.BlockSpec((tq, QM_W), lambda b, i: (b * nq + i, 0)),
                  pl.BlockSpec((seq_len, WIDTH), lambda b, i: (b, 0)),
                  pl.BlockSpec((seq_len, WIDTH), lambda b, i: (b, 0)),
                  pl.BlockSpec((3 * KEY_BLOCK, 2 * KEY_BLOCK), lambda b, i: (0, 0))],
        out_specs=pl.BlockSpec((tq, WIDTH), lambda b, i: (b * nq + i, 0)),
        out_shape=jax.ShapeDtypeStruct((n_seq * seq_len, WIDTH), BF16),
        scratch_shapes=[pltpu.VMEM((tq, WIDTH), F32), pltpu.VMEM((N_HEADS, tq, LANES), F32)],
        compiler_params=_cparams(("arbitrary", "arbitrary")),
        name="attn_prompt",
    )(qm, kb, vb, tri)


def _pack_r(r_ref, tq):
    lane = _iota((tq, LANES), 1)
    rp = jnp.zeros((tq, LANES), F32)
    for h in range(N_HEADS):
        rp = jnp.where(lane == h, r_ref[h], rp)
    return rp


def _attn_sample_body(qm_ref, kn_ref, vn_ref, ck_ref, cv_ref, tri_ref, acc_out, rp_out, acc_ref, r_ref):
    tq = qm_ref.shape[0]
    acc_ref[...] = jnp.zeros_like(acc_ref)
    r_ref[...] = jnp.zeros_like(r_ref)
    q_heads = _query_heads(qm_ref[...])
    kn, vn = kn_ref[...], vn_ref[...]
    head = lambda x: (lambda h: x[:, HEAD_DIM * h:HEAD_DIM * (h + 1)])
    causal = _iota((tq, tq), 1) < _iota((tq, tq), 0)
    rmax = _sb_block_heads(q_heads, head(kn), head(vn), causal, acc_ref, r_ref, tri_ref)

    for j in reversed(range(ck_ref.shape[4] // KEY_BLOCK)):
        rmax = lax.cond(rmax > LOG_CUTOFF,
                        lambda j=j: _sb_block_heads(q_heads, _cache_heads(ck_ref, j), _cache_heads(cv_ref, j), None,
                                                    acc_ref, r_ref, tri_ref, keys_on_lanes=True),
                        lambda rmax=rmax: rmax)
    acc_out[...] = acc_ref[...]
    rp_out[...] = _pack_r(r_ref, tq)


def _attn_sample(qm, kb, vb, cache_k, cache_v, tri, layer, row0, n_seq, tq):
    past = cache_k.shape[4]
    win = min(SAMPLE_WINDOW, past)
    blk0 = row0 // tq
    cur = lambda b: (blk0 + b, 0)
    cblock = pl.BlockSpec((1, 1, N_HEADS, HEAD_DIM, win), lambda b: (layer, b, 0, 0, past // win - 1))
    return pl.pallas_call(
        _attn_sample_body,
        grid=(n_seq,),
        in_specs=[pl.BlockSpec((tq, QM_W), cur), pl.BlockSpec((tq, WIDTH), cur), pl.BlockSpec((tq, WIDTH), cur),
                  cblock, cblock,
                  pl.BlockSpec((3 * KEY_BLOCK, 2 * KEY_BLOCK), lambda b: (0, 0))],
        out_specs=[pl.BlockSpec((tq, WIDTH), lambda b: (b, 0)), pl.BlockSpec((tq, LANES), lambda b: (b, 0))],
        out_shape=[jax.ShapeDtypeStruct((n_seq * tq, WIDTH), F32), jax.ShapeDtypeStruct((n_seq * tq, LANES), F32)],
        scratch_shapes=[pltpu.VMEM((tq, WIDTH), F32), pltpu.VMEM((N_HEADS, tq, LANES), F32)],
        compiler_params=_cparams(("arbitrary",)),
        name="attn_sample",
    )(qm, kb, vb, cache_k, cache_v, tri)


def _attn_older_body(qm_ref, ck_ref, cv_ref, acc_in, rp_in, tri_ref, acc_out, acc_ref, r_ref):
    s = pl.program_id(1)
    tq = qm_ref.shape[0]

    @pl.when(s == 0)
    def _():
        acc_ref[...] = acc_in[...]
        rp = rp_in[...]
        for h in range(N_HEADS):
            r_ref[h] = jnp.broadcast_to(rp[:, h:h + 1], (tq, LANES))

    rmax = jnp.max(r_ref[0])
    for h in range(1, N_HEADS):
        rmax = jnp.maximum(rmax, jnp.max(r_ref[h]))

    @pl.when(rmax > LOG_CUTOFF)
    def _():
        _sb_block_heads(_query_heads(qm_ref[...]), _cache_heads(ck_ref, 0), _cache_heads(cv_ref, 0), None,
                        acc_ref, r_ref, tri_ref, keys_on_lanes=True)

    @pl.when(s == pl.num_programs(1) - 1)
    def _():
        acc_out[...] = acc_ref[...]


def _attn_older(qm, cache_k, cache_v, acc, rp, tri, layer, row0, n_seq, tq):
    past = cache_k.shape[4]
    win = min(SAMPLE_WINDOW, past)
    nb = (past - win) // KEY_BLOCK
    blk0 = row0 // tq
    cblock = pl.BlockSpec((1, 1, N_HEADS, HEAD_DIM, KEY_BLOCK), lambda b, s: (layer, b, 0, 0, nb - 1 - s))
    return pl.pallas_call(
        _attn_older_body,
        grid=(n_seq, nb),
        in_specs=[pl.BlockSpec((tq, QM_W), lambda b, s: (blk0 + b, 0)), cblock, cblock,
                  pl.BlockSpec((tq, WIDTH), lambda b, s: (b, 0)), pl.BlockSpec((tq, LANES), lambda b, s: (b, 0)),
                  pl.BlockSpec((3 * KEY_BLOCK, 2 * KEY_BLOCK), lambda b, s: (0, 0))],
        out_specs=pl.BlockSpec((tq, WIDTH), lambda b, s: (b, 0)),
        out_shape=jax.ShapeDtypeStruct((n_seq * tq, WIDTH), F32),
        scratch_shapes=[pltpu.VMEM((tq, WIDTH), F32), pltpu.VMEM((N_HEADS, tq, LANES), F32)],
        compiler_params=_cparams(("arbitrary", "arbitrary")),
        name="attn_older",
    )(qm, cache_k, cache_v, acc, rp, tri)


def _causal_conv(xp_ref, n, w_ref, b_ref):
    y = b_ref[...]
    for k in range(CONV_W):
        y = y + xp_ref[pl.ds(SUBLANES - (CONV_W - 1) + k, n), :] * w_ref[k:k + 1, :]
    return y


def _lru_rows(xp_ref, a_ref, b_ref, n, h0, gate, cw_ref, cb_ref, wa_ref, ba_ref, wx_ref, bx_ref, lam_ref):
    xc = _causal_conv(xp_ref, n, cw_ref, cb_ref)
    xcb = xc.astype(BF16)
    r = jax.nn.sigmoid(_dot(xcb, wa_ref[...]) + ba_ref[...])
    ig = jax.nn.sigmoid(_dot(xcb, wx_ref[...]) + bx_ref[...])
    log_a = LRU_C * r * _log_sigmoid(lam_ref[...])
    a_ref[0:n, :] = jnp.exp(log_a)
    th = jnp.tanh(log_a)
    b_ref[0:n, :] = jnp.sqrt(-2.0 * th / (1.0 - th)) * (ig * xc)
    row = _iota((SUBLANES, WIDTH), 0)

    def group(g, h):
        off = pl.multiple_of(g * SUBLANES, SUBLANES)
        a = a_ref[pl.ds(off, SUBLANES), :]
        b = b_ref[pl.ds(off, SUBLANES), :]
        for s in (1, 2, 4):
            a_prev = jnp.where(row >= s, pltpu.roll(a, s, 0), 1.0)
            b_prev = jnp.where(row >= s, pltpu.roll(b, s, 0), 0.0)
            b = b + a * b_prev
            a = a * a_prev
        hs = a * h + b
        b_ref[pl.ds(off, SUBLANES), :] = hs
        return hs[SUBLANES - 1:SUBLANES, :]

    h_last = lax.fori_loop(0, n // SUBLANES, group, h0)
    y = b_ref[0:n, :] * jax.nn.gelu(gate, approximate=True)
    return y, h_last


def _lru_body(n_ptiles, tps, seg, xb_ref, gb_ref, conv0_ref, h0_ref, cw_ref, cb_ref, wa_ref, ba_ref, wx_ref,
              bx_ref, lam_ref, o_ref, convp_ref, hp_ref, convs_ref, hs_ref, xp_ref, a_ref, b_ref, hc_ref):
    i = pl.program_id(0)
    tt = xb_ref.shape[0]
    params = (cw_ref, cb_ref, wa_ref, ba_ref, wx_ref, bx_ref, lam_ref)

    @pl.when(i < n_ptiles)
    def _():
        @pl.when(i % tps == 0)
        def _():
            xp_ref[0:SUBLANES, :] = jnp.zeros((SUBLANES, WIDTH), F32)
            hc_ref[...] = jnp.zeros_like(hc_ref)

        xp_ref[SUBLANES:SUBLANES + tt, :] = xb_ref[...]
        y, h_last = _lru_rows(xp_ref, a_ref, b_ref, tt, hc_ref[...], gb_ref[...], *params)
        o_ref[...] = y.astype(BF16)
        hc_ref[...] = h_last
        xp_ref[0:SUBLANES, :] = xp_ref[tt:tt + SUBLANES, :]

        @pl.when(i % tps == tps - 1)
        def _():
            convp_ref[0] = xp_ref[SUBLANES - (CONV_W - 1):SUBLANES, :]
            hp_ref[0] = h_last

    @pl.when(i >= n_ptiles)
    def _():
        for s in range(tt // seg):
            xp_ref[SUBLANES - (CONV_W - 1):SUBLANES, :] = conv0_ref[s]
            xp_ref[SUBLANES:SUBLANES + seg, :] = xb_ref[s * seg:(s + 1) * seg, :]
            y, h_last = _lru_rows(xp_ref, a_ref, b_ref, seg, h0_ref[s:s + 1, :],
                                  gb_ref[s * seg:(s + 1) * seg, :], *params)
            o_ref[s * seg:(s + 1) * seg, :] = y.astype(BF16)
            convs_ref[s] = xp_ref[seg + SUBLANES - (CONV_W - 1):seg + SUBLANES, :]
            hs_ref[s:s + 1, :] = h_last


def _mixer_specs(n_ptiles, tps, spt):
    pseq = lambda i: jnp.minimum(i, n_ptiles - 1) // tps
    stile = lambda i: jnp.maximum(i - n_ptiles, 0)
    return pseq, stile


def _lru(rest, conv0, h0, cw, cb, wa, ba, wx, bx, lam, n_pseq, seq_len, n_sseq, seg):
    t = rest.shape[0]
    tt = TOKEN_TILE
    tps = seq_len // tt
    n_ptiles = n_pseq * tps
    spt = tt // seg
    pseq, stile = _mixer_specs(n_ptiles, tps, spt)
    fixed = lambda i: (0, 0)
    vec = pl.BlockSpec((1, WIDTH), fixed)
    return pl.pallas_call(
        functools.partial(_lru_body, n_ptiles, tps, seg),
        grid=(t // tt,),
        in_specs=[pl.BlockSpec((tt, WIDTH), lambda i: (i, REST_XB // WIDTH)),
                  pl.BlockSpec((tt, WIDTH), lambda i: (i, REST_GB // WIDTH)),
                  pl.BlockSpec((spt, CONV_W - 1, WIDTH), lambda i: (stile(i), 0, 0)),
                  pl.BlockSpec((spt, WIDTH), lambda i: (stile(i), 0)),
                  pl.BlockSpec((CONV_W, WIDTH), fixed), vec,
                  pl.BlockSpec((WIDTH, WIDTH), fixed), vec, pl.BlockSpec((WIDTH, WIDTH), fixed), vec, vec],
        out_specs=[pl.BlockSpec((tt, WIDTH), lambda i: (i, 0)),
                   pl.BlockSpec((1, CONV_W - 1, WIDTH), lambda i: (pseq(i), 0, 0)),
                   pl.BlockSpec((1, 1, WIDTH), lambda i: (pseq(i), 0, 0)),
                   pl.BlockSpec((spt, CONV_W - 1, WIDTH), lambda i: (stile(i), 0, 0)),
                   pl.BlockSpec((spt, WIDTH), lambda i: (stile(i), 0))],
        out_shape=[jax.ShapeDtypeStruct((t, WIDTH), BF16),
                   jax.ShapeDtypeStruct((n_pseq, CONV_W - 1, WIDTH), F32),
                   jax.ShapeDtypeStruct((n_pseq, 1, WIDTH), F32),
                   jax.ShapeDtypeStruct((n_sseq, CONV_W - 1, WIDTH), F32),
                   jax.ShapeDtypeStruct((n_sseq, WIDTH), F32)],
        scratch_shapes=[pltpu.VMEM((tt + SUBLANES, WIDTH), F32), pltpu.VMEM((tt, WIDTH), F32),
                        pltpu.VMEM((tt, WIDTH), F32), pltpu.VMEM((1, WIDTH), F32)],
        compiler_params=_cparams(("arbitrary",)),
        name="rglru",
    )(rest, rest, conv0, h0, cw, cb, wa, ba, wx, bx, lam)


def _ssd_chunk(xc_ref, z_ref, dtr_ref, y_ref, hst_ref, r0, q, dtb_ref, alog_ref, dsk_ref, gn_ref, exp_ref,
               tril_ref, eye_ref):
    xs = xc_ref[r0:r0 + q, 0:WIDTH]
    bm = xc_ref[r0:r0 + q, WIDTH:WIDTH + 2 * SSM_STATE].astype(BF16)
    cm = xc_ref[r0:r0 + q, WIDTH + 2 * SSM_STATE:SSM_CONV_DIM].astype(BF16)
    dt = jax.nn.softplus(dtr_ref[r0:r0 + q, :] + dtb_ref[...])
    da = dt * (-jnp.exp(alog_ref[...]))
    a_cum = _sel_left(tril_ref[0:q, 0:q], da)
    a_exp = _sel_right(a_cum, exp_ref[...])
    dt_exp = _sel_right(dt, exp_ref[...])
    a_cum_t = sum(_dot_nt(eye_ref[...], p) for p in _split3(a_cum))
    a_last = a_cum[q - 1:q, :]
    xdt = xs * dt_exp
    xdtb = xdt.astype(BF16)
    xw = (xdt * jnp.exp(a_exp[q - 1:q, :] - a_exp)).astype(BF16)
    ea = jnp.exp(a_exp)
    causal = _iota((q, q), 0) >= _iota((q, q), 1)
    for g in range(2):
        bg = bm[:, SSM_STATE * g:SSM_STATE * (g + 1)]
        cg = cm[:, SSM_STATE * g:SSM_STATE * (g + 1)]
        cb = _dot_nt(cg, bg)
        for e in range(4 * g, 4 * g + 4):
            hs = slice(HEAD_DIM * e, HEAD_DIM * (e + 1))
            seg = a_cum[:, e:e + 1] - a_cum_t[e:e + 1, :]
            m = (cb * jnp.exp(jnp.where(causal, seg, -1e30))).astype(BF16)
            h_old = hst_ref[e]
            y = _dot(m, xdtb[:, hs]) + _dot_nt(cg, h_old.astype(BF16)) * ea[:, hs]
            y_ref[r0:r0 + q, hs] = y
            decay = jnp.exp(jnp.broadcast_to(a_last[:, e:e + 1], (1, SSM_STATE)))
            hst_ref[e] = decay * h_old + _dot_tn(xw[:, hs], bg)
    y = y_ref[r0:r0 + q, :] + dsk_ref[...] * xs
    y = y * _silu(z_ref[r0:r0 + q, :])
    half = WIDTH // 2
    outs = []
    for g in range(2):
        yg = y[:, half * g:half * (g + 1)]
        outs.append(yg * lax.rsqrt(jnp.mean(yg * yg, axis=-1, keepdims=True) + EPS))
    return jnp.concatenate(outs, axis=1) * gn_ref[...]


def _ssd_body(n_ptiles, tps, seg, q_prompt, xbc_ref, z_ref, dtr_ref, conv0_ref, h0_ref, cw_ref, cb_ref, dtb_ref,
              alog_ref, dsk_ref, gn_ref, exp_ref, tril_ref, eye_ref, o_ref, convp_ref, hp_ref, convs_ref, hs_ref,
              xp_ref, xc_ref, y_ref, hst_ref):
    i = pl.program_id(0)
    tt = xbc_ref.shape[0]
    params = (dtb_ref, alog_ref, dsk_ref, gn_ref, exp_ref, tril_ref, eye_ref)
    tail = slice(SUBLANES - (CONV_W - 1), SUBLANES)

    @pl.when(i < n_ptiles)
    def _():
        @pl.when(i % tps == 0)
        def _():
            xp_ref[0:SUBLANES, :] = jnp.zeros((SUBLANES, SSM_CONV_DIM), F32)
            hst_ref[...] = jnp.zeros_like(hst_ref)

        xp_ref[SUBLANES:SUBLANES + tt, :] = xbc_ref[...]
        xc_ref[...] = _silu(_causal_conv(xp_ref, tt, cw_ref, cb_ref))
        for c in range(tt // q_prompt):
            r0 = c * q_prompt
            o_ref[r0:r0 + q_prompt, :] = _ssd_chunk(xc_ref, z_ref, dtr_ref, y_ref, hst_ref, r0, q_prompt,
                                                    *params).astype(BF16)
        xp_ref[0:SUBLANES, :] = xp_ref[tt:tt + SUBLANES, :]

        @pl.when(i % tps == tps - 1)
        def _():
            convp_ref[0] = xp_ref[tail, :]
            hp_ref[0] = hst_ref[...]

    @pl.when(i >= n_ptiles)
    def _():
        for s in range(tt // seg):
            r0 = s * seg
            xp_ref[tail, :] = conv0_ref[s]
            xp_ref[SUBLANES:SUBLANES + seg, :] = xbc_ref[r0:r0 + seg, :]
            xc_ref[r0:r0 + seg, :] = _silu(_causal_conv(xp_ref, seg, cw_ref, cb_ref))
            hst_ref[...] = h0_ref[s]
            o_ref[r0:r0 + seg, :] = _ssd_chunk(xc_ref, z_ref, dtr_ref, y_ref, hst_ref, r0, seg,
                                               *params).astype(BF16)
            convs_ref[s] = xp_ref[seg + SUBLANES - (CONV_W - 1):seg + SUBLANES, :]
            hs_ref[s] = hst_ref[...]


def _ssd(rest, conv0, h0, cw, cb, dtb, alog, dsk, gn, expand, tril, eye, n_pseq, seq_len, n_sseq, seg):
    t = rest.shape[0]
    tt = TOKEN_TILE
    tps = seq_len // tt
    n_ptiles = n_pseq * tps
    spt = tt // seg
    pseq, stile = _mixer_specs(n_ptiles, tps, spt)
    fixed = lambda i: (0, 0)
    hshape = (N_HEADS, HEAD_DIM, SSM_STATE)
    return pl.pallas_call(
        functools.partial(_ssd_body, n_ptiles, tps, seg, SSD_CHUNK),
        grid=(t // tt,),
        in_specs=[pl.BlockSpec((tt, SSM_CONV_DIM), lambda i: (i, REST_XBC // SSM_CONV_DIM)),
                  pl.BlockSpec((tt, WIDTH), lambda i: (i, REST_ZC // WIDTH)),
                  pl.BlockSpec((tt, LANES), lambda i: (i, REST_DT // LANES)),
                  pl.BlockSpec((spt, CONV_W - 1, SSM_CONV_DIM), lambda i: (stile(i), 0, 0)),
                  pl.BlockSpec((spt,) + hshape, lambda i: (stile(i), 0, 0, 0)),
                  pl.BlockSpec((CONV_W, SSM_CONV_DIM), fixed), pl.BlockSpec((1, SSM_CONV_DIM), fixed),
                  pl.BlockSpec((1, LANES), fixed), pl.BlockSpec((1, LANES), fixed),
                  pl.BlockSpec((1, WIDTH), fixed), pl.BlockSpec((1, WIDTH), fixed),
                  pl.BlockSpec((LANES, WIDTH), fixed), pl.BlockSpec((KEY_BLOCK, KEY_BLOCK), fixed),
                  pl.BlockSpec((SUBLANES, LANES), fixed)],
        out_specs=[pl.BlockSpec((tt, WIDTH), lambda i: (i, 0)),
                   pl.BlockSpec((1, CONV_W - 1, SSM_CONV_DIM), lambda i: (pseq(i), 0, 0)),
                   pl.BlockSpec((1,) + hshape, lambda i: (pseq(i), 0, 0, 0)),
                   pl.BlockSpec((spt, CONV_W - 1, SSM_CONV_DIM), lambda i: (stile(i), 0, 0)),
                   pl.BlockSpec((spt,) + hshape, lambda i: (stile(i), 0, 0, 0))],
        out_shape=[jax.ShapeDtypeStruct((t, WIDTH), BF16),
                   jax.ShapeDtypeStruct((n_pseq, CONV_W - 1, SSM_CONV_DIM), F32),
                   jax.ShapeDtypeStruct((n_pseq,) + hshape, F32),
                   jax.ShapeDtypeStruct((n_sseq, CONV_W - 1, SSM_CONV_DIM), F32),
                   jax.ShapeDtypeStruct((n_sseq,) + hshape, F32)],
        scratch_shapes=[pltpu.VMEM((tt + SUBLANES, SSM_CONV_DIM), F32), pltpu.VMEM((tt, SSM_CONV_DIM), F32),
                        pltpu.VMEM((tt, WIDTH), F32), pltpu.VMEM(hshape, F32)],
        compiler_params=_cparams(("arbitrary",)),
        name="ssd",
    )(rest, rest, rest, conv0, h0, cw, cb, dtb, alog, dsk, gn, expand, tril, eye)


def _route(lt, be_ref, bg_ref):
    n = lt.shape[1]
    le = lt[0:N_EXPERTS, :] + be_ref[:, 0:1]
    lg = lt[N_EXPERTS:N_EXPERTS + SUBLANES, :] + bg_ref[:, 0:1]
    gmax = jnp.max(lg, axis=0, keepdims=True)
    gi = _iota((SUBLANES, n), 0)
    g_sel = jnp.min(jnp.where(lg == gmax, gi, SUBLANES), axis=0, keepdims=True)
    p_sel = 1.0 / jnp.sum(jnp.exp(lg - gmax), axis=0, keepdims=True)
    ei = _iota((N_EXPERTS, n), 0)
    m1 = jnp.where((ei >> 3) == g_sel, le, -jnp.inf)
    v1 = jnp.max(m1, axis=0, keepdims=True)
    i1 = jnp.min(jnp.where(m1 == v1, ei, N_EXPERTS), axis=0, keepdims=True)
    m2 = jnp.where(ei == i1, -jnp.inf, m1)
    v2 = jnp.max(m2, axis=0, keepdims=True)
    i2 = jnp.min(jnp.where(m2 == v2, ei, N_EXPERTS), axis=0, keepdims=True)
    e2 = jnp.exp(v2 - v1)
    w1 = p_sel / (1.0 + e2)
    w2 = w1 * e2
    r = _iota((SUBLANES, n), 0)
    out = jnp.where(r == 0, i1.astype(F32), 0.0)
    out = jnp.where(r == 1, i2.astype(F32), out)
    out = jnp.where(r == 2, w1, out)
    return jnp.where(r == 3, w2, out)


def _merge_body(n_ptiles, h_ref, oap_ref, oas_ref, ob_ref, oc_ref, gmix_ref, wg_ref, bg_ref, wbo_ref, wout_ref,
                gffn_ref, wr_ref, bre_ref, brg_ref, h1_ref, u2_ref, route_ref, ya_ref):
    i = pl.program_id(0)
    x = h_ref[...]
    ub = _rmsnorm(x, gmix_ref[...]).astype(BF16)

    @pl.when(i < n_ptiles)
    def _():
        ya_ref[...] = _dot(oap_ref[...], wbo_ref[0:WIDTH, :])

    @pl.when(i >= n_ptiles)
    def _():
        ya_ref[...] = _dot(oas_ref[...].astype(BF16), wbo_ref[0:WIDTH, :])

    def gate(b):
        return jax.nn.sigmoid(_dot(ub, wg_ref[:, D_MODEL * b:D_MODEL * (b + 1)])
                              + bg_ref[:, D_MODEL * b:D_MODEL * (b + 1)])

    merged = gate(0) * ya_ref[...]
    merged = merged + gate(1) * _dot(ob_ref[...], wbo_ref[WIDTH:2 * WIDTH, :])
    merged = merged + gate(2) * _dot(oc_ref[...], wbo_ref[2 * WIDTH:3 * WIDTH, :])
    h1 = x + _dot(merged.astype(BF16), wout_ref[...])
    h1_ref[...] = h1
    u2 = _rmsnorm(h1, gffn_ref[...])
    u2_ref[...] = u2
    lt = _dot_nt(wr_ref[...].astype(BF16), u2.astype(BF16))
    route_ref[...] = _route(lt, bre_ref, brg_ref)


def _merge(h, oa_p, oa_s, ob, oc, gmix, wg, bg, wbo, wout, gffn, wr, bre, brg):
    t = h.shape[0]
    tm = PROJ_TILE
    n_ptiles = oa_p.shape[0] // tm
    row = lambda i: (i, 0)
    fixed = lambda i: (0, 0)
    full = lambda a: pl.BlockSpec(a.shape, fixed, pipeline_mode=pl.Buffered(1))
    return pl.pallas_call(
        functools.partial(_merge_body, n_ptiles),
        grid=(t // tm,),
        in_specs=[pl.BlockSpec((tm, D_MODEL), row),
                  pl.BlockSpec((tm, WIDTH), lambda i: (jnp.minimum(i, n_ptiles - 1), 0)),
                  pl.BlockSpec((tm, WIDTH), lambda i: (jnp.maximum(i - n_ptiles, 0), 0)),
                  pl.BlockSpec((tm, WIDTH), row), pl.BlockSpec((tm, WIDTH), row),
                  full(gmix), full(wg), full(bg), full(wbo), full(wout), full(gffn), full(wr), full(bre), full(brg)],
        out_specs=[pl.BlockSpec((tm, D_MODEL), row), pl.BlockSpec((tm, D_MODEL), row),
                   pl.BlockSpec((SUBLANES, tm), lambda i: (0, i))],
        out_shape=[jax.ShapeDtypeStruct((t, D_MODEL), F32), jax.ShapeDtypeStruct((t, D_MODEL), F32),
                   jax.ShapeDtypeStruct((SUBLANES, t), F32)],
        scratch_shapes=[pltpu.VMEM((tm, D_MODEL), F32)],
        compiler_params=_cparams(("arbitrary",)),
        name="merge",
    )(h, oa_p, oa_s, ob, oc, gmix, wg, bg, wbo, wout, gffn, wr, bre, brg)


def _rank_body(route_ref, before_ref, ltri_ref, dest_ref, meta_ref, cnt_ref, start_ref):
    p = pl.program_id(0)
    i = pl.program_id(1)
    tm = route_ref.shape[1]
    ei = _iota((N_EXPERTS, tm), 0)
    hot0 = ei == route_ref[0:1, :].astype(I32)
    hot1 = ei == route_ref[1:2, :].astype(I32)
    both = jnp.where(jnp.logical_or(hot0, hot1), 1.0, 0.0)
    tile_cnt = jnp.broadcast_to(jnp.sum(both, axis=1, keepdims=True), (N_EXPERTS, LANES))

    @pl.when(jnp.logical_and(p == 0, i == 0))
    def _():
        cnt_ref[...] = jnp.zeros_like(cnt_ref)

    @pl.when(jnp.logical_and(p == 1, i == 0))
    def _():
        cnt = cnt_ref[...]
        padded = jnp.floor((cnt + (EXPERT_BLOCK - 1)) * (1.0 / EXPERT_BLOCK)) * EXPERT_BLOCK
        start = _sel_left(ltri_ref[...], padded)
        start_ref[...] = start
        lane = _iota((N_EXPERTS, LANES), 1)
        meta_ref[...] = jnp.where(lane == 0, cnt, jnp.where(lane == 1, start, 0.0))
        cnt_ref[...] = jnp.zeros_like(cnt_ref)

    @pl.when(p == 1)
    def _():
        prior = _dot(both.astype(BF16), before_ref[...])
        slot = prior + (start_ref[:, 0:1] + cnt_ref[:, 0:1])
        d0 = jnp.sum(jnp.where(hot0, slot, 0.0), axis=0, keepdims=True)
        d1 = jnp.sum(jnp.where(hot1, slot, 0.0), axis=0, keepdims=True)
        dest_ref[0] = jnp.concatenate([d0, d1], axis=0).astype(I32)

    cnt_ref[...] = cnt_ref[...] + tile_cnt


def _rank(route, before, ltri):
    t = route.shape[1]
    tm = TOKEN_TILE
    nt = t // tm
    return pl.pallas_call(
        _rank_body,
        grid=(2, nt),
        in_specs=[pl.BlockSpec((SUBLANES, tm), lambda p, i: (0, i)),
                  pl.BlockSpec((tm, tm), lambda p, i: (0, 0)),
                  pl.BlockSpec((N_EXPERTS, N_EXPERTS), lambda p, i: (0, 0))],
        out_specs=[pl.BlockSpec((1, 2, tm), lambda p, i: (i * p, 0, 0)),
                   pl.BlockSpec((N_EXPERTS, LANES), lambda p, i: (0, 0))],
        out_shape=[jax.ShapeDtypeStruct((nt, 2, tm), I32), jax.ShapeDtypeStruct((N_EXPERTS, LANES), F32)],
        scratch_shapes=[pltpu.VMEM((N_EXPERTS, LANES), F32), pltpu.VMEM((N_EXPERTS, LANES), F32)],
        compiler_params=_cparams(("arbitrary", "arbitrary")),
        name="moe_rank",
    )(route, before, ltri)


def _row_copy(src_ref, src_row, dst_ref, dst_row, sem):
    return pltpu.make_async_copy(src_ref.at[pl.ds(src_row, 1), :], dst_ref.at[pl.ds(dst_row, 1), :], sem)


def _dispatch_body(dest_ref, u_ref, xs_in_ref, xs_ref, sem):
    del xs_in_ref
    tm = u_ref.shape[0]

    def start(t, c):
        for k in range(2):
            _row_copy(u_ref, t, xs_ref, dest_ref[0, k, t], sem).start(priority=k)
        return c

    def wait(t, c):
        for k in range(2):
            _row_copy(u_ref, t, xs_ref, dest_ref[0, k, t], sem).wait()
        return c

    lax.fori_loop(0, tm, start, 0, unroll=ROW_DMA_UNROLL)
    lax.fori_loop(0, tm, wait, 0, unroll=ROW_DMA_UNROLL)


def _dispatch(dest, u2, xs_zero):
    t = u2.shape[0]
    tm = dest.shape[2]
    return pl.pallas_call(
        _dispatch_body,
        grid=(t // tm,),
        in_specs=[pl.BlockSpec((1, 2, tm), lambda i: (i, 0, 0), memory_space=pltpu.SMEM),
                  pl.BlockSpec((tm, D_MODEL), lambda i: (i, 0)),
                  pl.BlockSpec(memory_space=pl.ANY)],
        out_specs=pl.BlockSpec(memory_space=pl.ANY),
        out_shape=jax.ShapeDtypeStruct(xs_zero.shape, F32),
        scratch_shapes=[pltpu.SemaphoreType.DMA(())],
        input_output_aliases={2: 0},
        compiler_params=_cparams(("arbitrary",), disable_bounds_checks=True),
        name="moe_dispatch",
    )(dest, u2, xs_zero)


def _experts_body(be_ref, nu_ref, x_ref, wg_ref, wu_ref, wd_ref, y_ref, wgb_ref, wub_ref, wdb_ref):
    i = pl.program_id(0)
    live = i < nu_ref[0]

    @pl.when(jnp.logical_or(i == 0, be_ref[i] != be_ref[jnp.maximum(i - 1, 0)]))
    def _():
        wgb_ref[...] = wg_ref[0, 0].astype(BF16)
        wub_ref[...] = wu_ref[0, 0].astype(BF16)
        wdb_ref[...] = wd_ref[0, 0].astype(BF16)

    @pl.when(live)
    def _():
        xb = x_ref[...].astype(BF16)
        a = _silu(_dot(xb, wgb_ref[...])) * _dot(xb, wub_ref[...])
        y_ref[...] = _dot(a.astype(BF16), wdb_ref[...])

    @pl.when(jnp.logical_not(live))
    def _():
        y_ref[...] = jnp.zeros_like(y_ref)


def _experts(blk_expert, n_used, xs, wg, wu, wd, layer):
    cap = xs.shape[0]
    nb = cap // EXPERT_BLOCK
    live = lambda i, nu: jnp.minimum(i, nu[0] - 1)
    rows = lambda i, be, nu: (live(i, nu), 0)
    wsel = lambda i, be, nu: (layer, be[live(i, nu)], 0, 0)
    return pl.pallas_call(
        _experts_body,
        grid_spec=pltpu.PrefetchScalarGridSpec(
            num_scalar_prefetch=2,
            grid=(nb,),
            in_specs=[pl.BlockSpec((EXPERT_BLOCK, D_MODEL), rows),
                      pl.BlockSpec((1, 1, D_MODEL, D_EXPERT), wsel), pl.BlockSpec((1, 1, D_MODEL, D_EXPERT), wsel),
                      pl.BlockSpec((1, 1, D_EXPERT, D_MODEL), wsel)],
            out_specs=pl.BlockSpec((EXPERT_BLOCK, D_MODEL), lambda i, be, nu: (i, 0)),
            scratch_shapes=[pltpu.VMEM((D_MODEL, D_EXPERT), BF16), pltpu.VMEM((D_MODEL, D_EXPERT), BF16),
                            pltpu.VMEM((D_EXPERT, D_MODEL), BF16)]),
        out_shape=jax.ShapeDtypeStruct((cap, D_MODEL), F32),
        compiler_params=_cparams(("arbitrary",)),
        name="moe_experts",
    )(blk_expert, n_used, xs, wg, wu, wd)


def _combine_body(n_ptiles, per, dest_ref, h_ref, route_ref, g_ref, y_hbm, *rest):
    *o_refs, ybuf_ref, sem = rest
    i = pl.program_id(0)
    tm = h_ref.shape[0]
    base = (i % per) * tm

    def start(t, c):
        for k in range(2):
            _row_copy(y_hbm, dest_ref[0, k, base + t], ybuf_ref.at[k], t, sem).start(priority=k)
        return c

    def wait(t, c):
        for k in range(2):
            _row_copy(y_hbm, dest_ref[0, k, base + t], ybuf_ref.at[k], t, sem).wait()
        return c

    lax.fori_loop(0, tm, start, 0, unroll=ROW_DMA_UNROLL)
    eye = _iota((tm, tm), 0) == _iota((tm, tm), 1)
    w0 = jnp.sum(jnp.where(eye, route_ref[2:3, :], 0.0), axis=1, keepdims=True)
    w1 = jnp.sum(jnp.where(eye, route_ref[3:4, :], 0.0), axis=1, keepdims=True)
    lax.fori_loop(0, tm, wait, 0, unroll=ROW_DMA_UNROLL)
    h2 = h_ref[...] + (w0 * ybuf_ref[0] + w1 * ybuf_ref[1])
    if len(o_refs) == 1:
        o_refs[0][...] = h2
    else:
        @pl.when(i < n_ptiles)
        def _():
            o_refs[0][...] = _rmsnorm(h2, g_ref[...])

        @pl.when(i >= n_ptiles)
        def _():
            o_refs[1][...] = _rmsnorm(h2, g_ref[...])


def _combine(dest, h1, route, g_final, y, t_p, final):
    t = h1.shape[0]
    tm = PROJ_TILE
    per = dest.shape[2] // tm
    n_ptiles = t_p // tm
    if final:
        out_specs = [pl.BlockSpec((tm, D_MODEL), lambda i: (jnp.minimum(i, n_ptiles - 1), 0)),
                     pl.BlockSpec((tm, D_MODEL), lambda i: (jnp.maximum(i - n_ptiles, 0), 0))]
        out_shape = [jax.ShapeDtypeStruct((t_p, D_MODEL), F32), jax.ShapeDtypeStruct((t - t_p, D_MODEL), F32)]
    else:
        out_specs = pl.BlockSpec((tm, D_MODEL), lambda i: (i, 0))
        out_shape = jax.ShapeDtypeStruct((t, D_MODEL), F32)
    return pl.pallas_call(
        functools.partial(_combine_body, n_ptiles, per),
        grid=(t // tm,),
        in_specs=[pl.BlockSpec((1, 2, dest.shape[2]), lambda i: (i // per, 0, 0), memory_space=pltpu.SMEM),
                  pl.BlockSpec((tm, D_MODEL), lambda i: (i, 0)),
                  pl.BlockSpec((SUBLANES, tm), lambda i: (0, i)),
                  pl.BlockSpec((1, D_MODEL), lambda i: (0, 0)),
                  pl.BlockSpec(memory_space=pl.ANY)],
        out_specs=out_specs,
        out_shape=out_shape,
        scratch_shapes=[pltpu.VMEM((2, tm, D_MODEL), F32), pltpu.SemaphoreType.DMA(())],
        compiler_params=_cparams(("arbitrary",), disable_bounds_checks=True),
        name="moe_combine",
    )(dest, h1, route, g_final, y)


def _constants():
    r = jnp.arange(3 * KEY_BLOCK)
    c = jnp.arange(2 * KEY_BLOCK)
    tri = jnp.where(c[None, :] < KEY_BLOCK, (r[:, None] % KEY_BLOCK) > c[None, :], True).astype(BF16)
    q = jnp.arange(KEY_BLOCK)
    tril = (q[:, None] >= q[None, :]).astype(BF16)
    lane = jnp.arange(WIDTH)
    expand = (jnp.arange(LANES)[:, None] == lane[None, :] // HEAD_DIM).astype(BF16)
    eye = (jnp.arange(SUBLANES)[:, None] == jnp.arange(LANES)[None, :]).astype(BF16)
    t = jnp.arange(TOKEN_TILE)
    before = (t[:, None] < t[None, :]).astype(BF16)
    e = jnp.arange(N_EXPERTS)
    ltri = (e[:, None] > e[None, :]).astype(BF16)
    return tri, tril, expand, eye, before, ltri


def _block_diag(w):
    n, d = w.shape[0], w.shape[1]
    out = jnp.zeros((n, d, n, d), w.dtype)
    out = out.at[jnp.arange(n), :, jnp.arange(n), :].set(w)
    return out.reshape(n * d, n * d)


def _proj_weight(w_in):
    wq = w_in[:, 0:WIDTH].reshape(D_MODEL, N_HEADS, 1, HEAD_DIM)
    half = (jnp.arange(N_HEADS) % 2)[None, :, None, None] == jnp.arange(2)[None, None, :, None]
    wqm = jnp.where(half, wq * (HEAD_DIM ** -0.5), 0.0).reshape(D_MODEL, QM_W)
    o = 3 * WIDTH
    xb, gb, zc = w_in[:, o:o + WIDTH], w_in[:, o + WIDTH:o + 2 * WIDTH], w_in[:, o + 2 * WIDTH:o + 3 * WIDTH]
    xbc = w_in[:, o + 3 * WIDTH:o + 3 * WIDTH + SSM_CONV_DIM]
    dt = jnp.pad(w_in[:, o + 3 * WIDTH + SSM_CONV_DIM:], ((0, 0), (0, LANES - N_HEADS)))
    return jnp.concatenate([wqm, w_in[:, WIDTH:3 * WIDTH], xbc, xb, gb, zc, dt], axis=1).astype(BF16)


def _pad_lanes(v):
    return jnp.pad(v, (0, LANES - v.shape[0]))[None, :]


def kernel(x_prompt, x_sample, cache_sb_k, cache_sb_v, state_lru_conv, state_lru_h, state_ssm_conv, state_ssm_h, g_mix, w_in, w_gate, b_gate, w_branch_out, w_out, lru_conv_w, lru_conv_b, lru_w_a, lru_b_a, lru_w_x, lru_b_x, lru_lambda, ssm_conv_w, ssm_conv_b, ssm_dt_bias, ssm_a_log, ssm_d, ssm_norm_g, g_ffn, w_router_group, b_router_group, w_router_expert, b_router_expert, w_expert_gate, w_expert_up, w_expert_down, g_final):
    n_p, s_p, _ = x_prompt.shape
    n_s, s_s, _ = x_sample.shape
    depth = w_in.shape[0]
    past = cache_sb_k.shape[2]
    t_p, t_s = n_p * s_p, n_s * s_s
    t = t_p + t_s
    assert s_p % TOKEN_TILE == 0 and TOKEN_TILE % s_s == 0 and t_s % TOKEN_TILE == 0
    assert s_s % SUBLANES == 0 and past % SAMPLE_WINDOW == 0

    tri, tril, expand, eye, before, ltri = _constants()
    cache_k = jnp.transpose(cache_sb_k, (0, 1, 3, 4, 2))
    cache_v = jnp.transpose(cache_sb_v, (0, 1, 3, 4, 2))
    n_blocks = -(-2 * t // EXPERT_BLOCK) + N_EXPERTS
    xs = jnp.zeros((n_blocks * EXPERT_BLOCK, D_MODEL), F32)

    h = jnp.concatenate([x_prompt.reshape(t_p, D_MODEL), x_sample.reshape(t_s, D_MODEL)], axis=0)
    states = []
    for l in range(depth):
        row = lambda a: a[l][None, :]
        qm, k_p, v_p, k_s, v_s, kb, vb, rest = _inproj(h, row(g_mix), _proj_weight(w_in[l]), n_p, s_p)

        oa_p = _attn_prompt(qm, kb, vb, tri, n_p, s_p)
        acc, rp = _attn_sample(qm, kb, vb, cache_k, cache_v, tri, l, t_p, n_s, s_s)
        if past > SAMPLE_WINDOW:
            older = functools.partial(_attn_older, qm, cache_k, cache_v, tri=tri, layer=l, row0=t_p, n_seq=n_s,
                                      tq=s_s)
            oa_s = lax.cond(jnp.max(rp[:, :N_HEADS]) > LOG_CUTOFF,
                            lambda a, r: older(acc=a, rp=r), lambda a, r: a, acc, rp)
        else:
            oa_s = acc

        ob, lconv_p, lh_p, lconv_s, lh_s = _lru(
            rest, state_lru_conv[l], state_lru_h[l], lru_conv_w[l], row(lru_conv_b),
            _block_diag(lru_w_a[l]).astype(BF16), row(lru_b_a), _block_diag(lru_w_x[l]).astype(BF16),
            row(lru_b_x), row(lru_lambda), n_p, s_p, n_s, s_s)

        oc, sconv_p, sh_p, sconv_s, sh_s = _ssd(
            rest, state_ssm_conv[l], state_ssm_h[l], ssm_conv_w[l], row(ssm_conv_b),
            _pad_lanes(ssm_dt_bias[l]), _pad_lanes(ssm_a_log[l]), jnp.repeat(ssm_d[l], HEAD_DIM)[None, :],
            row(ssm_norm_g), expand, tril, eye, n_p, s_p, n_s, s_s)

        w_r = jnp.concatenate([w_router_expert[l].T, w_router_group[l].T,
                               jnp.zeros((SUBLANES - N_GROUPS, D_MODEL), F32)], axis=0)
        b_re = jnp.broadcast_to(b_router_expert[l][:, None], (N_EXPERTS, LANES))
        b_rg = jnp.broadcast_to(jnp.concatenate([b_router_group[l], jnp.full((SUBLANES - N_GROUPS,), -1e30, F32)])[:, None],
                                (SUBLANES, LANES))
        h1, u2, route = _merge(h, oa_p, oa_s, ob, oc, row(g_mix), w_gate[l].astype(BF16), row(b_gate),
                               w_branch_out[l].astype(BF16), w_out[l].astype(BF16), row(g_ffn), w_r, b_re, b_rg)

        dest, meta = _rank(route, before, ltri)
        counts, starts = meta[:, 0], meta[:, 1]
        ends = starts + jnp.ceil(counts / EXPERT_BLOCK) * EXPERT_BLOCK
        blk_row = (jnp.arange(n_blocks) * EXPERT_BLOCK).astype(F32)
        blk_expert = jnp.minimum(jnp.sum(ends[None, :] <= blk_row[:, None], axis=1), N_EXPERTS - 1).astype(I32)
        n_used = (ends[N_EXPERTS - 1:] / EXPERT_BLOCK).astype(I32)

        xs = _dispatch(dest, u2, xs)
        y = _experts(blk_expert, n_used, xs, w_expert_gate, w_expert_up, w_expert_down, l)
        h = _combine(dest, h1, route, g_final[None, :], y, t_p, final=(l == depth - 1))

        head = lambda a, n, s: a.reshape(n, s, N_HEADS, HEAD_DIM)
        states.append((k_p, v_p, lconv_p, lh_p[:, 0], sconv_p, sh_p,
                       head(k_s, n_s, s_s), head(v_s, n_s, s_s), lconv_s, lh_s, sconv_s, sh_s))

    stacked = [jnp.stack([st[j] for st in states], axis=0) for j in range(12)]
    for j in range(2):
        kt = stacked[j].reshape(depth, n_p, N_HEADS, HEAD_DIM, s_p)
        stacked[j] = jnp.transpose(kt, (0, 1, 4, 2, 3))
    y_p, y_s = h
    return (y_p.reshape(n_p, s_p, D_MODEL), y_s.reshape(n_s, s_s, D_MODEL)) + tuple(stacked)
```
